```python
import math
import jax, jax.numpy as jnp
from jax import lax
import numpy as np

D_MODEL = 2048
BATCH = 4
SEQ = 2048
DEPTH = 4

ATTN_WIDTH = D_MODEL // 2
CONV_WIDTH = D_MODEL - ATTN_WIDTH
HEAD_DIM = 128
N_HEADS = ATTN_WIDTH // HEAD_DIM
IN_COLS = 3 * ATTN_WIDTH + 2 * CONV_WIDTH
BLOCK = 256
TOP_BLOCKS = 3
Q_CHUNK = 16
CONV_K = 31
N_BUCKETS = 32
MAX_DISTANCE = 128
N_GROUPS = 4
EXPERTS_PER_GROUP = 8
N_EXPERTS = N_GROUPS * EXPERTS_PER_GROUP
EXPERT_TOP_K = 2
D_EXPERT = 256
PLE_DIM = 256
ALPHA = (2 * DEPTH) ** 0.25
BETA = (8 * DEPTH) ** -0.25
LN_EPS = 1e-5

kernel_name = "hymba_moba_conformer_hmoe_deepnorm"


def layer_norm(x, g, b):
    xf = x.astype(jnp.float32)
    mu = xf.mean(-1, keepdims=True)
    var = jnp.square(xf - mu).mean(-1, keepdims=True)
    y = (xf - mu) * lax.rsqrt(var + LN_EPS) * g.astype(jnp.float32) + b.astype(jnp.float32)
    return y.astype(x.dtype)


def t5_bucket(rel):
    n = jnp.maximum(rel, 0)
    max_exact = N_BUCKETS // 2
    nf = jnp.maximum(n, max_exact).astype(jnp.float32)
    large = max_exact + (jnp.log(nf / max_exact) / math.log(MAX_DISTANCE / max_exact)
                         * (N_BUCKETS - max_exact)).astype(jnp.int32)
    large = jnp.minimum(large, N_BUCKETS - 1)
    return jnp.where(n < max_exact, n, large)


def moba_attention(q, k, v, rel_bias):
    B, S = q.shape[0], q.shape[1]
    n_blocks = -(-S // BLOCK)
    s_pad = n_blocks * BLOCK
    pad = ((0, 0), (0, s_pad - S), (0, 0), (0, 0))
    q = jnp.pad(q, pad).transpose(0, 2, 1, 3)
    k = jnp.pad(k, pad).transpose(0, 2, 1, 3)
    v = jnp.pad(v, pad).transpose(0, 2, 1, 3)
    kb = k.reshape(B, N_HEADS, n_blocks, BLOCK, HEAD_DIM)
    vb = v.reshape(B, N_HEADS, n_blocks, BLOCK, HEAD_DIM)
    k_mean = kb.astype(jnp.float32).mean(axis=3)
    n_chunks = s_pad // Q_CHUNK
    qc = q.reshape(B, N_HEADS, n_chunks, Q_CHUNK, HEAD_DIM).transpose(2, 0, 1, 3, 4)
    k_top = min(TOP_BLOCKS, n_blocks)
    scale = HEAD_DIM ** -0.5
    head_ids = jnp.arange(N_HEADS)
    block_ids = jnp.arange(n_blocks)
    gather = jax.vmap(jax.vmap(lambda blocks, idx: blocks[idx]))

    def one_chunk(args):
        q_blk, c = args
        q_pos = c * Q_CHUNK + jnp.arange(Q_CHUNK)
        cur = (c * Q_CHUNK) // BLOCK
        gate = jnp.einsum('bhqd,bhnd->bhqn', q_blk.astype(jnp.float32), k_mean)
        gate = jnp.where(block_ids < cur, gate, -jnp.inf)
        _, sel = lax.top_k(gate, k_top)
        sel_valid = sel < cur
        k_sel = gather(kb, sel)
        v_sel = gather(vb, sel)
        s_sel = jnp.einsum('bhqd,bhqkjd->bhqkj', q_blk, k_sel,
                           preferred_element_type=jnp.float32) * scale
        k_pos_sel = sel[..., None] * BLOCK + jnp.arange(BLOCK)
        bucket_sel = t5_bucket(q_pos[None, None, :, None, None] - k_pos_sel)
        s_sel = s_sel + rel_bias.T[head_ids[None, :, None, None, None], bucket_sel].astype(jnp.float32)
        s_sel = jnp.where(sel_valid[..., None], s_sel, -jnp.inf)
        k_own = lax.dynamic_index_in_dim(kb, cur, axis=2, keepdims=False)
        v_own = lax.dynamic_index_in_dim(vb, cur, axis=2, keepdims=False)
        s_own = jnp.einsum('bhqd,bhjd->bhqj', q_blk, k_own,
                           preferred_element_type=jnp.float32) * scale
        rel_own = q_pos[:, None] - (cur * BLOCK + jnp.arange(BLOCK))[None, :]
        s_own = s_own + rel_bias[t5_bucket(rel_own)].transpose(2, 0, 1)[None].astype(jnp.float32)
        s_own = jnp.where(rel_own >= 0, s_own, -jnp.inf)
        logits = jnp.concatenate([s_sel.reshape(B, N_HEADS, Q_CHUNK, k_top * BLOCK), s_own], -1)
        prob = jax.nn.softmax(logits, axis=-1)
        p_sel = prob[..., :k_top * BLOCK].reshape(B, N_HEADS, Q_CHUNK, k_top, BLOCK).astype(v.dtype)
        p_own = prob[..., k_top * BLOCK:].astype(v.dtype)
        out = (jnp.einsum('bhqkj,bhqkjd->bhqd', p_sel, v_sel)
               + jnp.einsum('bhqj,bhjd->bhqd', p_own, v_own))
        return out.astype(q.dtype)

    out = lax.map(one_chunk, (qc, jnp.arange(n_chunks)))
    out = out.transpose(1, 0, 3, 2, 4).reshape(B, s_pad, N_HEADS * HEAD_DIM)
    return out[:, :S]


def conformer_conv(a, g, conv_w, conv_b, ln_g, ln_b):
    u = a * jax.nn.sigmoid(g)
    y = lax.conv_general_dilated(u, conv_w[:, None, :].astype(u.dtype), window_strides=(1,),
                                 padding=((CONV_K - 1, 0),),
                                 dimension_numbers=('NWC', 'WIO', 'NWC'),
                                 feature_group_count=CONV_WIDTH) + conv_b
    return jax.nn.silu(layer_norm(y, ln_g, ln_b))


def hier_moe(x, wg, bg, we, be, w_gu, w_down):
    B, S, D = x.shape
    t = x.reshape(-1, D)
    T = t.shape[0]
    g_logits = (t @ wg + bg).astype(jnp.float32)
    g_prob = jax.nn.softmax(g_logits, axis=-1)
    g_top = jnp.argmax(g_logits, axis=-1)
    p_group = jnp.take_along_axis(g_prob, g_top[:, None], axis=1)[:, 0]
    e_logits = (t @ we + be).astype(jnp.float32).reshape(T, N_GROUPS, EXPERTS_PER_GROUP)
    e_in = jnp.take_along_axis(e_logits, g_top[:, None, None], axis=1)[:, 0]
    e_prob = jax.nn.softmax(e_in, axis=-1)
    top_p, top_i = lax.top_k(e_prob, EXPERT_TOP_K)
    top_p = top_p / top_p.sum(-1, keepdims=True)
    expert_ids = g_top[:, None] * EXPERTS_PER_GROUP + top_i
    weights = p_group[:, None] * top_p
    combine = (jax.nn.one_hot(expert_ids, N_EXPERTS, dtype=jnp.float32) * weights[..., None]).sum(1)
    h = jnp.einsum('td,edf->tef', t, w_gu)
    act = jax.nn.silu(h[..., :D_EXPERT]) * h[..., D_EXPERT:] * combine[..., None].astype(t.dtype)
    out = jnp.einsum('tef,efd->td', act, w_down)
    return out.reshape(B, S, D)


def setup_inputs(seed: int = 0) -> dict:
    key = jax.random.key(seed)
    ks = jax.random.split(key, 24)
    f32 = jnp.float32
    nrm = lambda k, shape, s: jax.random.normal(k, shape, f32) * s
    return {
        "x": nrm(ks[0], (BATCH, SEQ, D_MODEL), 1.0),
        "p": nrm(ks[1], (DEPTH, BATCH, SEQ, PLE_DIM), 1.0),
        "w_in": nrm(ks[2], (DEPTH, D_MODEL, IN_COLS), D_MODEL ** -0.5),
        "conv_w": nrm(ks[3], (DEPTH, CONV_K, CONV_WIDTH), CONV_K ** -0.5),
        "conv_b": nrm(ks[4], (DEPTH, CONV_WIDTH), 0.02),
        "conv_ln_g": 1.0 + nrm(ks[5], (DEPTH, CONV_WIDTH), 0.02),
        "conv_ln_b": nrm(ks[6], (DEPTH, CONV_WIDTH), 0.02),
        "w_out": nrm(ks[7], (DEPTH, ATTN_WIDTH + CONV_WIDTH, D_MODEL), BETA * D_MODEL ** -0.5),
        "rel_bias": nrm(ks[8], (N_BUCKETS, N_HEADS), 0.5),
        "ln1_g": 1.0 + nrm(ks[9], (DEPTH, D_MODEL), 0.02),
        "ln1_b": nrm(ks[10], (DEPTH, D_MODEL), 0.02),
        "router_g_w": nrm(ks[11], (DEPTH, D_MODEL, N_GROUPS), D_MODEL ** -0.5),
        "router_g_b": nrm(ks[12], (DEPTH, N_GROUPS), 0.01),
        "router_e_w": nrm(ks[13], (DEPTH, D_MODEL, N_EXPERTS), D_MODEL ** -0.5),
        "router_e_b": nrm(ks[14], (DEPTH, N_EXPERTS), 0.01),
        "expert_w_gu": nrm(ks[15], (DEPTH, N_EXPERTS, D_MODEL, 2 * D_EXPERT), D_MODEL ** -0.5),
        "expert_w_down": nrm(ks[16], (DEPTH, N_EXPERTS, D_EXPERT, D_MODEL), BETA * D_EXPERT ** -0.5),
        "ln2_g": 1.0 + nrm(ks[17], (DEPTH, D_MODEL), 0.02),
        "ln2_b": nrm(ks[18], (DEPTH, D_MODEL), 0.02),
        "ple_w": nrm(ks[19], (DEPTH, PLE_DIM, D_MODEL), 0.5 * PLE_DIM ** -0.5),
        "ple_gate_w": nrm(ks[20], (DEPTH, D_MODEL, D_MODEL), D_MODEL ** -0.5),
    }


def reference(x, p, w_in, conv_w, conv_b, conv_ln_g, conv_ln_b, w_out, rel_bias,
              ln1_g, ln1_b, router_g_w, router_g_b, router_e_w, router_e_b,
              expert_w_gu, expert_w_down, ln2_g, ln2_b, ple_w, ple_gate_w):
    B, S, _ = x.shape
    a0, a1, a2, a3 = ATTN_WIDTH, 2 * ATTN_WIDTH, 3 * ATTN_WIDTH, 3 * ATTN_WIDTH + CONV_WIDTH
    for i in range(DEPTH):
        proj = x @ w_in[i]
        q = proj[..., :a0].reshape(B, S, N_HEADS, HEAD_DIM)
        k = proj[..., a0:a1].reshape(B, S, N_HEADS, HEAD_DIM)
        v = proj[..., a1:a2].reshape(B, S, N_HEADS, HEAD_DIM)
        attn = moba_attention(q, k, v, rel_bias)
        conv = conformer_conv(proj[..., a2:a3], proj[..., a3:], conv_w[i], conv_b[i],
                              conv_ln_g[i], conv_ln_b[i])
        mix = jnp.concatenate([attn, conv], axis=-1) @ w_out[i]
        h = layer_norm(ALPHA * x + mix, ln1_g[i], ln1_b[i])
        f = hier_moe(h, router_g_w[i], router_g_b[i], router_e_w[i], router_e_b[i],
                     expert_w_gu[i], expert_w_down[i])
        h2 = layer_norm(ALPHA * h + f, ln2_g[i], ln2_b[i])
        x = h2 + jax.nn.sigmoid(h2 @ ple_gate_w[i]) * (p[i] @ ple_w[i])
    return x
```

```python
import functools
import math

import numpy as np
import jax
import jax.numpy as jnp
from jax import lax
from jax.experimental import pallas as pl
from jax.experimental.pallas import tpu as pltpu

F32 = jnp.float32
BF16 = jnp.bfloat16

HEAD_DIM = 128
BLOCK = 256
TOP_BLOCKS = 3
CONV_K = 31
N_BUCKETS = 32
MAX_DISTANCE = 128
N_GROUPS = 4
EXPERTS_PER_GROUP = 8
N_EXPERTS = N_GROUPS * EXPERTS_PER_GROUP
LN_EPS = 1e-5
NEG = -1e30
ROUTER_LANES = 128
VMEM_LIMIT = 56 * 1024 * 1024


def _bucket_thresholds():
    n = np.arange(0, 4 * MAX_DISTANCE)
    max_exact = N_BUCKETS // 2
    nf = np.maximum(n, max_exact).astype(np.float32)
    large = max_exact + (np.log(nf / np.float32(max_exact)) / np.float32(math.log(MAX_DISTANCE / max_exact))
                         * np.float32(N_BUCKETS - max_exact)).astype(np.int32)
    bucket = np.where(n < max_exact, n, np.minimum(large, N_BUCKETS - 1))
    assert np.all(np.diff(bucket) >= 0) and bucket[-1] == N_BUCKETS - 1
    return [int(np.argmax(bucket >= b)) for b in range(N_BUCKETS)]


_BUCKET_START = _bucket_thresholds()


def _params(**kw):
    return pltpu.CompilerParams(vmem_limit_bytes=VMEM_LIMIT, **kw)


def _bias_kernel(rb_ref, out_ref):
    h = pl.program_id(0)
    kj = lax.broadcasted_iota(jnp.int32, (BLOCK, BLOCK), 0)
    qi = lax.broadcasted_iota(jnp.int32, (BLOCK, BLOCK), 1)
    for d in range(3):
        rel = qi - kj + d * BLOCK
        val = jnp.full((BLOCK, BLOCK), rb_ref[0, h], F32)
        for b in range(1, N_BUCKETS):
            val = jnp.where(rel >= _BUCKET_START[b], rb_ref[b, h], val)
        if d == 0:
            val = jnp.where(rel >= 0, val, NEG)
        out_ref[d] = val


def _bias_tiles(rel_bias):
    n_heads = rel_bias.shape[1]
    assert _BUCKET_START[-1] <= BLOCK + 1
    return pl.pallas_call(
        _bias_kernel,
        grid=(n_heads,),
        in_specs=[pl.BlockSpec(memory_space=pltpu.SMEM)],
        out_specs=pl.BlockSpec((None, 3, BLOCK, BLOCK), lambda h: (h, 0, 0, 0)),
        out_shape=jax.ShapeDtypeStruct((n_heads, 3, BLOCK, BLOCK), F32),
        name="t5_bias_tiles",
    )(rel_bias)


def _mm_kernel(x_ref, w_ref, o_ref):
    o_ref[...] = jnp.dot(x_ref[...], w_ref[...], preferred_element_type=F32).astype(o_ref.dtype)


def _proj_qk(xb, w_in, layer, tm=512, tn=1024):
    t, d = xb.shape
    n_out = 2 * tn
    return pl.pallas_call(
        _mm_kernel,
        grid=(n_out // tn, t // tm),
        in_specs=[pl.BlockSpec((tm, d), lambda j, i: (i, 0)),
                  pl.BlockSpec((None, d, tn), lambda j, i: (layer, 0, j))],
        out_specs=pl.BlockSpec((tm, tn), lambda j, i: (i, j)),
        out_shape=jax.ShapeDtypeStruct((t, n_out), BF16),
        compiler_params=_params(),
        name="proj_qk",
    )(xb, w_in)


def _proj_vt_kernel(x_ref, w_ref, o_ref):
    r = jnp.dot(x_ref[...], w_ref[...], preferred_element_type=F32)
    for s in range(o_ref.shape[0]):
        o_ref[s] = r[s * BLOCK:(s + 1) * BLOCK, :].T.astype(o_ref.dtype)


def _proj_vt(xb, w_in, layer, batch, width, col_block, tm=512):
    t, d = xb.shape
    seq = t // batch
    nb = seq // BLOCK
    per_b = seq // tm
    bpt = tm // BLOCK
    return pl.pallas_call(
        _proj_vt_kernel,
        grid=(t // tm,),
        in_specs=[pl.BlockSpec((tm, d), lambda i: (i, 0)),
                  pl.BlockSpec((None, d, width), lambda i: (layer, 0, col_block))],
        out_specs=pl.BlockSpec((None, bpt, width, BLOCK), lambda i: (i // per_b, i % per_b, 0, 0)),
        out_shape=jax.ShapeDtypeStruct((batch, nb, width, BLOCK), BF16),
        compiler_params=_params(),
        name="proj_vt",
    )(xb, w_in)


def _proj_glu_kernel(x_ref, wa_ref, wg_ref, o_ref):
    x = x_ref[...]
    a = jnp.dot(x, wa_ref[...], preferred_element_type=F32)
    g = jnp.dot(x, wg_ref[...], preferred_element_type=F32)
    o_ref[...] = a * jax.nn.sigmoid(g)


def _proj_glu(xb, w_in, layer, a_col, width, tm=512, tn=512):
    t, d = xb.shape
    a_blk = a_col // tn
    g_blk = (a_col + width) // tn
    return pl.pallas_call(
        _proj_glu_kernel,
        grid=(width // tn, t // tm),
        in_specs=[pl.BlockSpec((tm, d), lambda j, i: (i, 0)),
                  pl.BlockSpec((None, d, tn), lambda j, i: (layer, 0, a_blk + j)),
                  pl.BlockSpec((None, d, tn), lambda j, i: (layer, 0, g_blk + j))],
        out_specs=pl.BlockSpec((tm, tn), lambda j, i: (i, j)),
        out_shape=jax.ShapeDtypeStruct((t, width), F32),
        compiler_params=_params(),
        name="proj_glu",
    )(xb, w_in, w_in)


def _attn_kernel(q_ref, k_ref, vt_ref, bias_ref, o_ref, kmean_ref, pen_ref, *, n_blocks, scale):
    c = pl.program_id(2)

    @pl.when(c == 0)
    def _():
        for n in range(n_blocks):
            kmean_ref[n:n + 1, :] = jnp.mean(k_ref[n * BLOCK:(n + 1) * BLOCK, :].astype(F32), axis=0, keepdims=True)

    q = q_ref[...]
    nt = (((1,), (1,)), ((), ()))
    gate = lax.dot_general(kmean_ref[...], q.astype(F32), nt, precision=lax.Precision.HIGHEST,
                           preferred_element_type=F32)
    row = lax.broadcasted_iota(jnp.int32, gate.shape, 0)
    gate_m = jnp.where(row < c, gate, -jnp.inf)
    for n in range(n_blocks):
        g_n = gate[n:n + 1, :]
        beats = (gate_m > g_n) | ((gate_m == g_n) & (row < n))
        rank = jnp.sum(beats.astype(F32), axis=0, keepdims=True)
        keep = (rank < TOP_BLOCKS) & (n < c)
        pen_ref[n:n + 1, :] = jnp.where(keep, 0.0, NEG)

    def scores(n, d):
        kb = k_ref[pl.ds(pl.multiple_of(n * BLOCK, BLOCK), BLOCK), :]
        s = lax.dot_general(kb, q, nt, preferred_element_type=F32)
        return s * scale + bias_ref[d]

    s = scores(c, 0)
    m0 = jnp.max(s, axis=0, keepdims=True)
    p = jnp.exp(s - m0)
    l0 = jnp.sum(p, axis=0, keepdims=True)
    acc0 = jnp.dot(vt_ref[c], p.astype(BF16), preferred_element_type=F32)

    def past(n, carry):
        m, l, acc = carry
        s = scores(n, jnp.minimum(c - n, 2)) + pen_ref[pl.ds(n, 1), :]
        m_new = jnp.maximum(m, jnp.max(s, axis=0, keepdims=True))
        alpha = jnp.exp(m - m_new)
        p = jnp.exp(s - m_new)
        l = l * alpha + jnp.sum(p, axis=0, keepdims=True)
        acc = acc * alpha + jnp.dot(vt_ref[n], p.astype(BF16), preferred_element_type=F32)
        return m_new, l, acc

    _, l, acc = lax.fori_loop(0, c, past, (m0, l0, acc0))
    o_ref[...] = (acc / l).T.astype(o_ref.dtype)


def _attention(qk, vt, bias, batch, n_heads):
    t = qk.shape[0]
    seq = t // batch
    nb = seq // BLOCK
    kern = functools.partial(_attn_kernel, n_blocks=nb, scale=HEAD_DIM ** -0.5)
    return pl.pallas_call(
        kern,
        grid=(batch, n_heads, nb),
        in_specs=[pl.BlockSpec((BLOCK, HEAD_DIM), lambda b, h, c: (b * nb + c, h)),
                  pl.BlockSpec((seq, HEAD_DIM), lambda b, h, c: (b, n_heads + h)),
                  pl.BlockSpec((None, nb, HEAD_DIM, BLOCK), lambda b, h, c: (b, 0, h, 0)),
                  pl.BlockSpec((None, 3, BLOCK, BLOCK), lambda b, h, c: (h, 0, 0, 0))],
        out_specs=pl.BlockSpec((BLOCK, HEAD_DIM), lambda b, h, c: (b * nb + c, h)),
        out_shape=jax.ShapeDtypeStruct((t, n_heads * HEAD_DIM), BF16),
        scratch_shapes=[pltpu.VMEM((nb, HEAD_DIM), F32), pltpu.VMEM((nb, BLOCK), F32)],
        compiler_params=_params(),
        name="moba_attention",
    )(qk, qk, vt, bias)


CONV_TILE = 256
CONV_HALO = 32
CONV_ROWS = 64
LANES = 128


def _conv_kernel(prev_ref, cur_ref, w_ref, b_ref, g_ref, beta_ref, o_ref, win_ref, y_ref):
    s = pl.program_id(1)
    win_ref[0:CONV_HALO, :] = jnp.where(s > 0, prev_ref[...], 0.0)
    win_ref[CONV_HALO:, :] = cur_ref[...]
    width = cur_ref.shape[1]
    first = CONV_HALO - (CONV_K - 1)
    for lc in range(width // LANES):
        ls = slice(lc * LANES, (lc + 1) * LANES)
        for rc in range(CONV_TILE // CONV_ROWS):
            r0 = rc * CONV_ROWS
            acc = jnp.broadcast_to(b_ref[:, ls], (CONV_ROWS, LANES))
            for k in range(CONV_K):
                acc = acc + win_ref[r0 + first + k:r0 + first + k + CONV_ROWS, ls] * w_ref[k:k + 1, ls]
            y_ref[r0:r0 + CONV_ROWS, ls] = acc
    y = y_ref[...]
    mu = jnp.mean(y, axis=-1, keepdims=True)
    yc = y - mu
    var = jnp.mean(yc * yc, axis=-1, keepdims=True)
    z = yc * lax.rsqrt(var + LN_EPS) * g_ref[...] + beta_ref[...]
    o_ref[...] = (z * jax.nn.sigmoid(z)).astype(o_ref.dtype)


def _conv_module(u, conv_w, conv_b, ln_g, ln_b, layer, batch):
    t, width = u.shape
    seq = t // batch
    tiles = seq // CONV_TILE
    halo_per_tile = CONV_TILE // CONV_HALO
    vec = pl.BlockSpec((None, 1, width), lambda b, s: (layer, 0, 0))
    return pl.pallas_call(
        _conv_kernel,
        grid=(batch, tiles),
        in_specs=[pl.BlockSpec((CONV_HALO, width),
                               lambda b, s: (jnp.maximum((b * tiles + s) * halo_per_tile - 1, 0), 0)),
                  pl.BlockSpec((CONV_TILE, width), lambda b, s: (b * tiles + s, 0)),
                  pl.BlockSpec((None, CONV_K, width), lambda b, s: (layer, 0, 0)),
                  vec, vec, vec],
        out_specs=pl.BlockSpec((CONV_TILE, width), lambda b, s: (b * tiles + s, 0)),
        out_shape=jax.ShapeDtypeStruct((t, width), BF16),
        scratch_shapes=[pltpu.VMEM((CONV_HALO + CONV_TILE, width), F32), pltpu.VMEM((CONV_TILE, width), F32)],
        compiler_params=_params(),
        name="conformer_conv",
    )(u, u, conv_w, conv_b, ln_g, ln_b)


def _layer_norm(y, g, b):
    mu = jnp.mean(y, axis=-1, keepdims=True)
    yc = y - mu
    var = jnp.mean(yc * yc, axis=-1, keepdims=True)
    return yc * lax.rsqrt(var + LN_EPS) * g + b


def _route(logits):
    lane = lax.broadcasted_iota(jnp.int32, logits.shape, 1)
    is_group = lane < N_GROUPS
    gl = jnp.where(is_group, logits, -jnp.inf)
    gmax = jnp.max(gl, axis=-1, keepdims=True)
    g_top = jnp.min(jnp.where(gl == gmax, lane, ROUTER_LANES), axis=-1, keepdims=True)
    p_group = 1.0 / jnp.sum(jnp.where(is_group, jnp.exp(gl - gmax), 0.0), axis=-1, keepdims=True)
    e_idx = lane - N_GROUPS
    in_group = (e_idx >= 0) & (e_idx < N_EXPERTS) & ((e_idx // EXPERTS_PER_GROUP) == g_top)
    el = jnp.where(in_group, logits, -jnp.inf)
    m1 = jnp.max(el, axis=-1, keepdims=True)
    i1 = jnp.min(jnp.where(el == m1, lane, ROUTER_LANES), axis=-1, keepdims=True)
    el2 = jnp.where(lane == i1, -jnp.inf, el)
    m2 = jnp.max(el2, axis=-1, keepdims=True)
    i2 = jnp.min(jnp.where(el2 == m2, lane, ROUTER_LANES), axis=-1, keepdims=True)
    e2 = jnp.exp(m2 - m1)
    w1 = p_group / (1.0 + e2)
    w2 = p_group * e2 / (1.0 + e2)
    return jnp.where(lane == i1, w1, 0.0) + jnp.where(lane == i2, w2, 0.0)


def _outproj_kernel(attn_ref, conv_ref, w_ref, x_ref, g_ref, b_ref, wr_ref, br_ref,
                    h_ref, hb_ref, comb_ref, *, alpha):
    ka = attn_ref.shape[1]
    mix = jnp.dot(attn_ref[...], w_ref[0:ka, :], preferred_element_type=F32)
    mix = mix + jnp.dot(conv_ref[...], w_ref[ka:, :], preferred_element_type=F32)
    h = _layer_norm(alpha * x_ref[...] + mix, g_ref[...], b_ref[...])
    h_ref[...] = h
    hb_ref[...] = h.astype(BF16)
    logits = jnp.dot(h, wr_ref[...], precision=lax.Precision.HIGHEST, preferred_element_type=F32) + br_ref[...]
    comb_ref[...] = _route(logits)


def _outproj(attn, conv, w_out, x, ln_g, ln_b, w_router, b_router, layer, alpha, tm=256):
    t, d = x.shape
    ka, kc = attn.shape[1], conv.shape[1]
    vec = pl.BlockSpec((None, 1, d), lambda i: (layer, 0, 0))
    row = lambda w: pl.BlockSpec((tm, w), lambda i: (i, 0))
    return pl.pallas_call(
        functools.partial(_outproj_kernel, alpha=alpha),
        grid=(t // tm,),
        in_specs=[row(ka), row(kc),
                  pl.BlockSpec((None, ka + kc, d), lambda i: (layer, 0, 0)),
                  row(d), vec, vec,
                  pl.BlockSpec((None, d, ROUTER_LANES), lambda i: (layer, 0, 0)),
                  pl.BlockSpec((None, 1, ROUTER_LANES), lambda i: (layer, 0, 0))],
        out_specs=[row(d), row(d), row(ROUTER_LANES)],
        out_shape=[jax.ShapeDtypeStruct((t, d), F32), jax.ShapeDtypeStruct((t, d), BF16),
                   jax.ShapeDtypeStruct((t, ROUTER_LANES), F32)],
        compiler_params=_params(),
        name="outproj_ln_router",
    )(attn, conv, w_out, x, ln_g, ln_b, w_router, b_router)


def _moe_kernel(h_ref, comb_ref, wgu_ref, wd_ref, o_ref):
    e = pl.program_id(1)

    @pl.when(e == 0)
    def _():
        o_ref[...] = jnp.zeros_like(o_ref)

    f = wd_ref.shape[0]
    hg = jnp.dot(h_ref[...], wgu_ref[...], preferred_element_type=F32)
    comb = comb_ref[...]
    lane = lax.broadcasted_iota(jnp.int32, comb.shape, 1)
    scale = jnp.sum(jnp.where(lane == e + N_GROUPS, comb, 0.0), axis=-1, keepdims=True)
    gate = hg[:, :f]
    act = gate * jax.nn.sigmoid(gate) * hg[:, f:] * scale
    o_ref[...] += jnp.dot(act.astype(BF16), wd_ref[...], preferred_element_type=F32)


def _moe(hb, comb, w_gu, w_down, layer, tm=1024):
    t, d = hb.shape
    n_exp, _, f2 = w_gu.shape[1:]
    f = w_down.shape[2]
    return pl.pallas_call(
        _moe_kernel,
        grid=(t // tm, n_exp),
        in_specs=[pl.BlockSpec((tm, d), lambda i, e: (i, 0)),
                  pl.BlockSpec((tm, ROUTER_LANES), lambda i, e: (i, 0)),
                  pl.BlockSpec((None, None, d, f2), lambda i, e: (layer, e, 0, 0)),
                  pl.BlockSpec((None, None, f, d), lambda i, e: (layer, e, 0, 0))],
        out_specs=pl.BlockSpec((tm, d), lambda i, e: (i, 0)),
        out_shape=jax.ShapeDtypeStruct((t, d), F32),
        compiler_params=_params(),
        name="moe_dense",
    )(hb, comb, w_gu, w_down)


def _ple_kernel(h_ref, f_ref, g_ref, b_ref, wgate_ref, p_ref, wp_ref, x_ref, xb_ref, *, alpha):
    h2 = _layer_norm(alpha * h_ref[...] + f_ref[...], g_ref[...], b_ref[...])
    gate = jax.nn.sigmoid(jnp.dot(h2.astype(BF16), wgate_ref[...], preferred_element_type=F32))
    pe = jnp.dot(p_ref[...].astype(BF16), wp_ref[...], preferred_element_type=F32)
    x_new = h2 + gate * pe
    x_ref[...] = x_new
    xb_ref[...] = x_new.astype(BF16)


def _ple(h, f, ln_g, ln_b, w_gate, p, w_ple, layer, alpha, tm=256):
    t, d = h.shape
    pd = p.shape[-1]
    vec = pl.BlockSpec((None, 1, d), lambda i: (layer, 0, 0))
    row = lambda w: pl.BlockSpec((tm, w), lambda i: (i, 0))
    return pl.pallas_call(
        functools.partial(_ple_kernel, alpha=alpha),
        grid=(t // tm,),
        in_specs=[row(d), row(d), vec, vec,
                  pl.BlockSpec((None, d, d), lambda i: (layer, 0, 0)),
                  pl.BlockSpec((None, tm, pd), lambda i: (layer, i, 0)),
                  pl.BlockSpec((None, pd, d), lambda i: (layer, 0, 0))],
        out_specs=[row(d), row(d)],
        out_shape=[jax.ShapeDtypeStruct((t, d), F32), jax.ShapeDtypeStruct((t, d), BF16)],
        compiler_params=_params(),
        name="ln2_ple",
    )(h, f, ln_g, ln_b, w_gate, p, w_ple)


def kernel(x, p, w_in, conv_w, conv_b, conv_ln_g, conv_ln_b, w_out, rel_bias, ln1_g, ln1_b, router_g_w, router_g_b, router_e_w, router_e_b, expert_w_gu, expert_w_down, ln2_g, ln2_b, ple_w, ple_gate_w):
    batch, seq, d = x.shape
    depth = w_in.shape[0]
    n_heads = rel_bias.shape[1]
    attn_w = n_heads * HEAD_DIM
    conv_width = conv_w.shape[2]
    t = batch * seq
    alpha = (2 * depth) ** 0.25

    w_in_b = w_in.astype(BF16)
    w_out_b = w_out.astype(BF16)
    w_gu_b = expert_w_gu.astype(BF16)
    w_down_b = expert_w_down.astype(BF16)
    w_gate_b = ple_gate_w.astype(BF16)
    w_ple_b = ple_w.astype(BF16)
    pad = ROUTER_LANES - N_GROUPS - N_EXPERTS
    w_router = jnp.pad(jnp.concatenate([router_g_w, router_e_w], axis=-1), ((0, 0), (0, 0), (0, pad)))
    b_router = jnp.pad(jnp.concatenate([router_g_b, router_e_b], axis=-1), ((0, 0), (0, pad)))[:, None, :]
    vec3 = lambda v: v[:, None, :]
    p2 = p.reshape(depth, t, p.shape[-1])

    bias = _bias_tiles(rel_bias)
    xf = x.reshape(t, d)
    xb = xf.astype(BF16)
    for i in range(depth):
        qk = _proj_qk(xb, w_in_b, i, tn=attn_w)
        vt = _proj_vt(xb, w_in_b, i, batch, attn_w, col_block=2)
        u = _proj_glu(xb, w_in_b, i, a_col=3 * attn_w, width=conv_width)
        attn = _attention(qk, vt, bias, batch, n_heads)
        conv = _conv_module(u, conv_w, vec3(conv_b), vec3(conv_ln_g), vec3(conv_ln_b), i, batch)
        h, hb, comb = _outproj(attn, conv, w_out_b, xf, vec3(ln1_g), vec3(ln1_b), w_router, b_router, i, alpha)
        f = _moe(hb, comb, w_gu_b, w_down_b, i)
        xf, xb = _ple(h, f, vec3(ln2_g), vec3(ln2_b), w_gate_b, p2, w_ple_b, i, alpha)
    return xf.reshape(batch, seq, d)
```

```python
import functools
import math

import numpy as np
import jax
import jax.numpy as jnp
from jax import lax
from jax.experimental import pallas as pl
from jax.experimental.pallas import tpu as pltpu

F32 = jnp.float32
BF16 = jnp.bfloat16

HEAD_DIM = 128
BLOCK = 256
TOP_BLOCKS = 3
CONV_K = 31
N_BUCKETS = 32
MAX_DISTANCE = 128
N_GROUPS = 4
EXPERTS_PER_GROUP = 8
N_EXPERTS = N_GROUPS * EXPERTS_PER_GROUP
LN_EPS = 1e-5
NEG = -1e30
ROUTER_LANES = 128
VMEM_LIMIT = 56 * 1024 * 1024


def _bucket_thresholds():
    n = np.arange(0, 4 * MAX_DISTANCE)
    max_exact = N_BUCKETS // 2
    nf = np.maximum(n, max_exact).astype(np.float32)
    large = max_exact + (np.log(nf / np.float32(max_exact)) / np.float32(math.log(MAX_DISTANCE / max_exact))
                         * np.float32(N_BUCKETS - max_exact)).astype(np.int32)
    bucket = np.where(n < max_exact, n, np.minimum(large, N_BUCKETS - 1))
    assert np.all(np.diff(bucket) >= 0) and bucket[-1] == N_BUCKETS - 1
    return [int(np.argmax(bucket >= b)) for b in range(N_BUCKETS)]


_BUCKET_START = _bucket_thresholds()


def _params(**kw):
    return pltpu.CompilerParams(vmem_limit_bytes=VMEM_LIMIT, **kw)


def _bias_kernel(rb_ref, out_ref):
    h = pl.program_id(0)
    kj = lax.broadcasted_iota(jnp.int32, (BLOCK, BLOCK), 0)
    qi = lax.broadcasted_iota(jnp.int32, (BLOCK, BLOCK), 1)
    for d in range(3):
        rel = qi - kj + d * BLOCK
        val = jnp.full((BLOCK, BLOCK), rb_ref[0, h], F32)
        for b in range(1, N_BUCKETS):
            val = jnp.where(rel >= _BUCKET_START[b], rb_ref[b, h], val)
        if d == 0:
            val = jnp.where(rel >= 0, val, NEG)
        out_ref[d] = val


def _bias_tiles(rel_bias):
    n_heads = rel_bias.shape[1]
    assert _BUCKET_START[-1] <= BLOCK + 1
    return pl.pallas_call(
        _bias_kernel,
        grid=(n_heads,),
        in_specs=[pl.BlockSpec(memory_space=pltpu.SMEM)],
        out_specs=pl.BlockSpec((None, 3, BLOCK, BLOCK), lambda h: (h, 0, 0, 0)),
        out_shape=jax.ShapeDtypeStruct((n_heads, 3, BLOCK, BLOCK), F32),
        name="t5_bias_tiles",
    )(rel_bias)


def _mm_kernel(x_ref, w_ref, o_ref):
    o_ref[...] = jnp.dot(x_ref[...], w_ref[...], preferred_element_type=F32).astype(o_ref.dtype)


def _proj_qk(xb, w_in, layer, tm=512, tn=1024):
    t, d = xb.shape
    n_out = 2 * tn
    return pl.pallas_call(
        _mm_kernel,
        grid=(n_out // tn, t // tm),
        in_specs=[pl.BlockSpec((tm, d), lambda j, i: (i, 0)),
                  pl.BlockSpec((None, d, tn), lambda j, i: (layer, 0, j))],
        out_specs=pl.BlockSpec((tm, tn), lambda j, i: (i, j)),
        out_shape=jax.ShapeDtypeStruct((t, n_out), BF16),
        compiler_params=_params(),
        name="proj_qk",
    )(xb, w_in)


def _proj_vt_kernel(x_ref, w_ref, o_ref):
    r = jnp.dot(x_ref[...], w_ref[...], preferred_element_type=F32)
    for s in range(o_ref.shape[0]):
        o_ref[s] = r[s * BLOCK:(s + 1) * BLOCK, :].T.astype(o_ref.dtype)


def _proj_vt(xb, w_in, layer, batch, width, col_block, tm=512):
    t, d = xb.shape
    seq = t // batch
    nb = seq // BLOCK
    per_b = seq // tm
    bpt = tm // BLOCK
    return pl.pallas_call(
        _proj_vt_kernel,
        grid=(t // tm,),
        in_specs=[pl.BlockSpec((tm, d), lambda i: (i, 0)),
                  pl.BlockSpec((None, d, width), lambda i: (layer, 0, col_block))],
        out_specs=pl.BlockSpec((None, bpt, width, BLOCK), lambda i: (i // per_b, i % per_b, 0, 0)),
        out_shape=jax.ShapeDtypeStruct((batch, nb, width, BLOCK), BF16),
        compiler_params=_params(),
        name="proj_vt",
    )(xb, w_in)


def _proj_glu_kernel(x_ref, wa_ref, wg_ref, o_ref):
    x = x_ref[...]
    a = jnp.dot(x, wa_ref[...], preferred_element_type=F32)
    g = jnp.dot(x, wg_ref[...], preferred_element_type=F32)
    o_ref[...] = a * jax.nn.sigmoid(g)


def _proj_glu(xb, w_in, layer, a_col, width, tm=512, tn=512):
    t, d = xb.shape
    a_blk = a_col // tn
    g_blk = (a_col + width) // tn
    return pl.pallas_call(
        _proj_glu_kernel,
        grid=(width // tn, t // tm),
        in_specs=[pl.BlockSpec((tm, d), lambda j, i: (i, 0)),
                  pl.BlockSpec((None, d, tn), lambda j, i: (layer, 0, a_blk + j)),
                  pl.BlockSpec((None, d, tn), lambda j, i: (layer, 0, g_blk + j))],
        out_specs=pl.BlockSpec((tm, tn), lambda j, i: (i, j)),
        out_shape=jax.ShapeDtypeStruct((t, width), F32),
        compiler_params=_params(),
        name="proj_glu",
    )(xb, w_in, w_in)


def _attn_kernel(q_ref, k_ref, vt_ref, bias_ref, o_ref, kmean_ref, pen_ref, *, n_blocks, scale):
    c = pl.program_id(2)

    @pl.when(c == 0)
    def _():
        for n in range(n_blocks):
            kmean_ref[n:n + 1, :] = jnp.mean(k_ref[n * BLOCK:(n + 1) * BLOCK, :].astype(F32), axis=0, keepdims=True)

    q = q_ref[...]
    nt = (((1,), (1,)), ((), ()))
    gate = lax.dot_general(kmean_ref[...], q.astype(F32), nt, precision=lax.Precision.HIGHEST,
                           preferred_element_type=F32)
    row = lax.broadcasted_iota(jnp.int32, gate.shape, 0)
    gate_m = jnp.where(row < c, gate, -jnp.inf)
    for n in range(n_blocks):
        g_n = gate[n:n + 1, :]
        beats = (gate_m > g_n) | ((gate_m == g_n) & (row < n))
        rank = jnp.sum(beats.astype(F32), axis=0, keepdims=True)
        keep = (rank < TOP_BLOCKS) & (n < c)
        pen_ref[n:n + 1, :] = jnp.where(keep, 0.0, NEG)

    def scores(n, d):
        kb = k_ref[pl.ds(pl.multiple_of(n * BLOCK, BLOCK), BLOCK), :]
        s = lax.dot_general(kb, q, nt, preferred_element_type=F32)
        return s * scale + bias_ref[d]

    s = scores(c, 0)
    m0 = jnp.max(s, axis=0, keepdims=True)
    p = jnp.exp(s - m0)
    l0 = jnp.sum(p, axis=0, keepdims=True)
    acc0 = jnp.dot(vt_ref[c], p.astype(BF16), preferred_element_type=F32)

    def past(n, carry):
        m, l, acc = carry
        s = scores(n, jnp.minimum(c - n, 2)) + pen_ref[pl.ds(n, 1), :]
        m_new = jnp.maximum(m, jnp.max(s, axis=0, keepdims=True))
        alpha = jnp.exp(m - m_new)
        p = jnp.exp(s - m_new)
        l = l * alpha + jnp.sum(p, axis=0, keepdims=True)
        acc = acc * alpha + jnp.dot(vt_ref[n], p.astype(BF16), preferred_element_type=F32)
        return m_new, l, acc

    _, l, acc = lax.fori_loop(0, c, past, (m0, l0, acc0))
    o_ref[...] = (acc / l).T.astype(o_ref.dtype)


def _attention(qk, vt, bias, batch, n_heads):
    t = qk.shape[0]
    seq = t // batch
    nb = seq // BLOCK
    kern = functools.partial(_attn_kernel, n_blocks=nb, scale=HEAD_DIM ** -0.5)
    return pl.pallas_call(
        kern,
        grid=(batch, n_heads, nb),
        in_specs=[pl.BlockSpec((BLOCK, HEAD_DIM), lambda b, h, c: (b * nb + c, h)),
                  pl.BlockSpec((seq, HEAD_DIM), lambda b, h, c: (b, n_heads + h)),
                  pl.BlockSpec((None, nb, HEAD_DIM, BLOCK), lambda b, h, c: (b, 0, h, 0)),
                  pl.BlockSpec((None, 3, BLOCK, BLOCK), lambda b, h, c: (h, 0, 0, 0))],
        out_specs=pl.BlockSpec((BLOCK, HEAD_DIM), lambda b, h, c: (b * nb + c, h)),
        out_shape=jax.ShapeDtypeStruct((t, n_heads * HEAD_DIM), BF16),
        scratch_shapes=[pltpu.VMEM((nb, HEAD_DIM), F32), pltpu.VMEM((nb, BLOCK), F32)],
        compiler_params=_params(),
        name="moba_attention",
    )(qk, qk, vt, bias)


CONV_TILE = 256
CONV_HALO = 32
CONV_ROWS = 64
LANES = 128


def _conv_kernel(prev_ref, cur_ref, w_ref, b_ref, g_ref, beta_ref, o_ref, win_ref, y_ref):
    s = pl.program_id(1)
    win_ref[0:CONV_HALO, :] = jnp.where(s > 0, prev_ref[...], 0.0)
    win_ref[CONV_HALO:, :] = cur_ref[...]
    width = cur_ref.shape[1]
    first = CONV_HALO - (CONV_K - 1)
    for lc in range(width // LANES):
        ls = slice(lc * LANES, (lc + 1) * LANES)
        for rc in range(CONV_TILE // CONV_ROWS):
            r0 = rc * CONV_ROWS
            acc = jnp.broadcast_to(b_ref[:, ls], (CONV_ROWS, LANES))
            for k in range(CONV_K):
                acc = acc + win_ref[r0 + first + k:r0 + first + k + CONV_ROWS, ls] * w_ref[k:k + 1, ls]
            y_ref[r0:r0 + CONV_ROWS, ls] = acc
    y = y_ref[...]
    mu = jnp.mean(y, axis=-1, keepdims=True)
    yc = y - mu
    var = jnp.mean(yc * yc, axis=-1, keepdims=True)
    z = yc * lax.rsqrt(var + LN_EPS) * g_ref[...] + beta_ref[...]
    o_ref[...] = (z * jax.nn.sigmoid(z)).astype(o_ref.dtype)


def _conv_module(u, conv_w, conv_b, ln_g, ln_b, layer, batch):
    t, width = u.shape
    seq = t // batch
    tiles = seq // CONV_TILE
    halo_per_tile = CONV_TILE // CONV_HALO
    vec = pl.BlockSpec((None, 1, width), lambda b, s: (layer, 0, 0))
    return pl.pallas_call(
        _conv_kernel,
        grid=(batch, tiles),
        in_specs=[pl.BlockSpec((CONV_HALO, width),
                               lambda b, s: (jnp.maximum((b * tiles + s) * halo_per_tile - 1, 0), 0)),
                  pl.BlockSpec((CONV_TILE, width), lambda b, s: (b * tiles + s, 0)),
                  pl.BlockSpec((None, CONV_K, width), lambda b, s: (layer, 0, 0)),
                  vec, vec, vec],
        out_specs=pl.BlockSpec((CONV_TILE, width), lambda b, s: (b * tiles + s, 0)),
        out_shape=jax.ShapeDtypeStruct((t, width), BF16),
        scratch_shapes=[pltpu.VMEM((CONV_HALO + CONV_TILE, width), F32), pltpu.VMEM((CONV_TILE, width), F32)],
        compiler_params=_params(),
        name="conformer_conv",
    )(u, u, conv_w, conv_b, ln_g, ln_b)


def _layer_norm(y, g, b):
    mu = jnp.mean(y, axis=-1, keepdims=True)
    yc = y - mu
    var = jnp.mean(yc * yc, axis=-1, keepdims=True)
    return yc * lax.rsqrt(var + LN_EPS) * g + b


def _route(logits):
    lane = lax.broadcasted_iota(jnp.int32, logits.shape, 1)
    is_group = lane < N_GROUPS
    gl = jnp.where(is_group, logits, -jnp.inf)
    gmax = jnp.max(gl, axis=-1, keepdims=True)
    g_top = jnp.min(jnp.where(gl == gmax, lane, ROUTER_LANES), axis=-1, keepdims=True)
    p_group = 1.0 / jnp.sum(jnp.where(is_group, jnp.exp(gl - gmax), 0.0), axis=-1, keepdims=True)
    e_idx = lane - N_GROUPS
    in_group = (e_idx >= 0) & (e_idx < N_EXPERTS) & ((e_idx // EXPERTS_PER_GROUP) == g_top)
    el = jnp.where(in_group, logits, -jnp.inf)
    m1 = jnp.max(el, axis=-1, keepdims=True)
    i1 = jnp.min(jnp.where(el == m1, lane, ROUTER_LANES), axis=-1, keepdims=True)
    el2 = jnp.where(lane == i1, -jnp.inf, el)
    m2 = jnp.max(el2, axis=-1, keepdims=True)
    i2 = jnp.min(jnp.where(el2 == m2, lane, ROUTER_LANES), axis=-1, keepdims=True)
    e2 = jnp.exp(m2 - m1)
    w1 = p_group / (1.0 + e2)
    w2 = p_group * e2 / (1.0 + e2)
    return i1, i2, w1, w2


def _outproj_kernel(attn_ref, conv_ref, w_ref, x_ref, g_ref, b_ref, wr_ref, br_ref,
                    h_ref, ri_ref, rw_ref, cnt_ref, carry_ref, *, alpha):
    i = pl.program_id(0)

    @pl.when(i == 0)
    def _():
        carry_ref[...] = jnp.zeros_like(carry_ref)

    ka = attn_ref.shape[1]
    mix = jnp.dot(attn_ref[...], w_ref[0:ka, :], preferred_element_type=F32)
    mix = mix + jnp.dot(conv_ref[...], w_ref[ka:, :], preferred_element_type=F32)
    h = _layer_norm(alpha * x_ref[...] + mix, g_ref[...], b_ref[...])
    h_ref[...] = h
    logits = jnp.dot(h, wr_ref[...], precision=lax.Precision.HIGHEST, preferred_element_type=F32) + br_ref[...]
    i1, i2, w1, w2 = _route(logits)

    tm = logits.shape[0]
    lane = lax.broadcasted_iota(jnp.int32, logits.shape, 1)
    onehot = jnp.where((lane == i1) | (lane == i2), 1.0, 0.0)
    rr = lax.broadcasted_iota(jnp.int32, (tm, tm), 0)
    cc = lax.broadcasted_iota(jnp.int32, (tm, tm), 1)
    earlier = jnp.where(cc < rr, 1.0, 0.0).astype(BF16)
    before = jnp.dot(earlier, onehot.astype(BF16), preferred_element_type=F32) + carry_ref[...]
    r1 = jnp.sum(jnp.where(lane == i1, before, 0.0), axis=-1, keepdims=True).astype(jnp.int32)
    r2 = jnp.sum(jnp.where(lane == i2, before, 0.0), axis=-1, keepdims=True).astype(jnp.int32)
    carry_ref[...] += jnp.sum(onehot, axis=0, keepdims=True)
    cnt_ref[...] = jnp.broadcast_to(carry_ref[...], cnt_ref.shape)
    ri_ref[...] = jnp.where(lane == 0, i1 - N_GROUPS,
                            jnp.where(lane == 1, i2 - N_GROUPS,
                                      jnp.where(lane == 2, r1, jnp.where(lane == 3, r2, 0))))
    rw_ref[...] = jnp.where(lane == 0, w1, jnp.where(lane == 1, w2, 0.0))


def _outproj(attn, conv, w_out, x, ln_g, ln_b, w_router, b_router, layer, alpha, tm=256):
    t, d = x.shape
    ka, kc = attn.shape[1], conv.shape[1]
    vec = pl.BlockSpec((None, 1, d), lambda i: (layer, 0, 0))
    row = lambda w: pl.BlockSpec((tm, w), lambda i: (i, 0))
    return pl.pallas_call(
        functools.partial(_outproj_kernel, alpha=alpha),
        grid=(t // tm,),
        in_specs=[row(ka), row(kc),
                  pl.BlockSpec((None, ka + kc, d), lambda i: (layer, 0, 0)),
                  row(d), vec, vec,
                  pl.BlockSpec((None, d, ROUTER_LANES), lambda i: (layer, 0, 0)),
                  pl.BlockSpec((None, 1, ROUTER_LANES), lambda i: (layer, 0, 0))],
        out_specs=[row(d), row(ROUTER_LANES), row(ROUTER_LANES),
                   pl.BlockSpec((8, ROUTER_LANES), lambda i: (0, 0))],
        out_shape=[jax.ShapeDtypeStruct((t, d), F32),
                   jax.ShapeDtypeStruct((t, ROUTER_LANES), jnp.int32),
                   jax.ShapeDtypeStruct((t, ROUTER_LANES), F32),
                   jax.ShapeDtypeStruct((8, ROUTER_LANES), F32)],
        scratch_shapes=[pltpu.VMEM((1, ROUTER_LANES), F32)],
        compiler_params=_params(dimension_semantics=("arbitrary",)),
        name="outproj_ln_router",
    )(attn, conv, w_out, x, ln_g, ln_b, w_router, b_router)


MOE_TILE = 256


def _moe_plan(route_i, cnt, n_rows):
    e = route_i[:, 0:2]
    rank = route_i[:, 2:4]
    counts = cnt[0, N_GROUPS:N_GROUPS + N_EXPERTS].astype(jnp.int32)
    ends = jnp.cumsum(counts)
    offs = ends - counts
    pos = (offs[e] + rank).reshape(-1)
    n_tiles = n_rows // MOE_TILE
    first_tile = offs // MOE_TILE
    n_items_e = jnp.where(counts > 0, (ends - 1) // MOE_TILE - first_tile + 1, 0)
    item_end = jnp.cumsum(n_items_e)
    item_start = item_end - n_items_e
    n_items = n_tiles + N_EXPERTS - 1
    j = jnp.arange(n_items, dtype=jnp.int32)
    valid = j < item_end[-1]
    jj = jnp.minimum(j, item_end[-1] - 1)
    ex = jnp.sum((jj[:, None] >= item_end[None, :]).astype(jnp.int32), axis=1)
    tile = first_tile[ex] + (jj - item_start[ex])
    lo = jnp.where(valid, jnp.maximum(offs[ex], tile * MOE_TILE) - tile * MOE_TILE, 0)
    hi = jnp.where(valid, jnp.minimum(ends[ex], (tile + 1) * MOE_TILE) - tile * MOE_TILE, 0)
    return pos, tile.astype(jnp.int32), ex.astype(jnp.int32), lo.astype(jnp.int32), hi.astype(jnp.int32)


def _dispatch_kernel(pos_ref, h_ref, xs_ref, sem):
    i = pl.program_id(0)
    tm = h_ref.shape[0]
    base = i * (2 * tm)
    copies = []
    for r in range(tm):
        for s in range(2):
            dst = pos_ref[base + 2 * r + s]
            cp = pltpu.make_async_copy(h_ref.at[pl.ds(r, 1), :], xs_ref.at[pl.ds(dst, 1), :], sem)
            cp.start()
            copies.append(cp)
    for cp in copies:
        cp.wait()


def _dispatch(pos, h, tm=256):
    t, d = h.shape
    return pl.pallas_call(
        _dispatch_kernel,
        grid_spec=pltpu.PrefetchScalarGridSpec(
            num_scalar_prefetch=1,
            grid=(t // tm,),
            in_specs=[pl.BlockSpec((tm, d), lambda i, pos: (i, 0))],
            out_specs=pl.BlockSpec(memory_space=pl.ANY),
            scratch_shapes=[pltpu.SemaphoreType.DMA]),
        out_shape=jax.ShapeDtypeStruct((2 * t, d), F32),
        compiler_params=_params(),
        name="moe_dispatch",
    )(pos, h)


def _gmm_kernel(tile_ref, exp_ref, lo_ref, hi_ref, xs_ref, wgu_ref, wd_ref, ys_ref):
    j = pl.program_id(0)
    f = wd_ref.shape[0]
    hg = jnp.dot(xs_ref[...].astype(BF16), wgu_ref[...].astype(BF16), preferred_element_type=F32)
    gate = hg[:, :f]
    row = lax.broadcasted_iota(jnp.int32, gate.shape, 0)
    mine = (row >= lo_ref[j]) & (row < hi_ref[j])
    act = jnp.where(mine, gate * jax.nn.sigmoid(gate) * hg[:, f:], 0.0)
    y = jnp.dot(act.astype(BF16), wd_ref[...].astype(BF16), preferred_element_type=F32)
    first = (j == 0) | (tile_ref[j] != tile_ref[jnp.maximum(j - 1, 0)])

    @pl.when(first)
    def _():
        ys_ref[...] = y

    @pl.when(jnp.logical_not(first))
    def _():
        ys_ref[...] += y


def _gmm(plan, xs, w_gu, w_down, layer):
    _, tile, ex, lo, hi = plan
    n_rows, d = xs.shape
    f2 = w_gu.shape[-1]
    f = w_down.shape[2]
    return pl.pallas_call(
        _gmm_kernel,
        grid_spec=pltpu.PrefetchScalarGridSpec(
            num_scalar_prefetch=4,
            grid=(tile.shape[0],),
            in_specs=[pl.BlockSpec((MOE_TILE, d), lambda j, tile, ex, lo, hi: (tile[j], 0)),
                      pl.BlockSpec((None, None, d, f2), lambda j, tile, ex, lo, hi: (layer, ex[j], 0, 0)),
                      pl.BlockSpec((None, None, f, d), lambda j, tile, ex, lo, hi: (layer, ex[j], 0, 0))],
            out_specs=pl.BlockSpec((MOE_TILE, d), lambda j, tile, ex, lo, hi: (tile[j], 0))),
        out_shape=jax.ShapeDtypeStruct((n_rows, d), F32),
        compiler_params=_params(dimension_semantics=("arbitrary",)),
        name="moe_grouped_matmul",
    )(tile, ex, lo, hi, xs, w_gu, w_down)


def _ple_kernel(pos_ref, h_ref, rw_ref, ys_ref, g_ref, b_ref, wgate_ref, p_ref, wp_ref, x_ref, xb_ref,
                ybuf, sem, *, alpha):
    i = pl.program_id(0)
    tm = h_ref.shape[0]
    base = i * (2 * tm)
    copies = []
    for r in range(tm):
        for s in range(2):
            src = pos_ref[base + 2 * r + s]
            cp = pltpu.make_async_copy(ys_ref.at[pl.ds(src, 1), :], ybuf.at[s, pl.ds(r, 1), :], sem)
            cp.start()
            copies.append(cp)
    for cp in copies:
        cp.wait()
    rw = rw_ref[...]
    f = rw[:, 0:1] * ybuf[0] + rw[:, 1:2] * ybuf[1]
    h2 = _layer_norm(alpha * h_ref[...] + f, g_ref[...], b_ref[...])
    gate = jax.nn.sigmoid(jnp.dot(h2.astype(BF16), wgate_ref[...], preferred_element_type=F32))
    pe = jnp.dot(p_ref[...].astype(BF16), wp_ref[...], preferred_element_type=F32)
    x_new = h2 + gate * pe
    x_ref[...] = x_new
    xb_ref[...] = x_new.astype(BF16)


def _ple(pos, h, rw, ys, ln_g, ln_b, w_gate, p, w_ple, layer, alpha, tm=256):
    t, d = h.shape
    pd = p.shape[-1]
    vec = pl.BlockSpec((None, 1, d), lambda i, pos: (layer, 0, 0))
    row = lambda w: pl.BlockSpec((tm, w), lambda i, pos: (i, 0))
    return pl.pallas_call(
        functools.partial(_ple_kernel, alpha=alpha),
        grid_spec=pltpu.PrefetchScalarGridSpec(
            num_scalar_prefetch=1,
            grid=(t // tm,),
            in_specs=[row(d), row(ROUTER_LANES), pl.BlockSpec(memory_space=pl.ANY), vec, vec,
                      pl.BlockSpec((None, d, d), lambda i, pos: (layer, 0, 0)),
                      pl.BlockSpec((None, tm, pd), lambda i, pos: (layer, i, 0)),
                      pl.BlockSpec((None, pd, d), lambda i, pos: (layer, 0, 0))],
            out_specs=[row(d), row(d)],
            scratch_shapes=[pltpu.VMEM((2, tm, d), F32), pltpu.SemaphoreType.DMA]),
        out_shape=[jax.ShapeDtypeStruct((t, d), F32), jax.ShapeDtypeStruct((t, d), BF16)],
        compiler_params=_params(),
        name="gather_ln2_ple",
    )(pos, h, rw, ys, ln_g, ln_b, w_gate, p, w_ple)


def kernel(x, p, w_in, conv_w, conv_b, conv_ln_g, conv_ln_b, w_out, rel_bias, ln1_g, ln1_b, router_g_w, router_g_b, router_e_w, router_e_b, expert_w_gu, expert_w_down, ln2_g, ln2_b, ple_w, ple_gate_w):
    batch, seq, d = x.shape
    depth = w_in.shape[0]
    n_heads = rel_bias.shape[1]
    attn_w = n_heads * HEAD_DIM
    conv_width = conv_w.shape[2]
    t = batch * seq
    alpha = (2 * depth) ** 0.25

    w_in_b = w_in.astype(BF16)
    w_out_b = w_out.astype(BF16)
    w_gate_b = ple_gate_w.astype(BF16)
    w_ple_b = ple_w.astype(BF16)
    pad = ROUTER_LANES - N_GROUPS - N_EXPERTS
    w_router = jnp.pad(jnp.concatenate([router_g_w, router_e_w], axis=-1), ((0, 0), (0, 0), (0, pad)))
    b_router = jnp.pad(jnp.concatenate([router_g_b, router_e_b], axis=-1), ((0, 0), (0, pad)))[:, None, :]
    vec3 = lambda v: v[:, None, :]
    p2 = p.reshape(depth, t, p.shape[-1])

    bias = _bias_tiles(rel_bias)
    xf = x.reshape(t, d)
    xb = xf.astype(BF16)
    for i in range(depth):
        qk = _proj_qk(xb, w_in_b, i, tn=attn_w)
        vt = _proj_vt(xb, w_in_b, i, batch, attn_w, col_block=2)
        u = _proj_glu(xb, w_in_b, i, a_col=3 * attn_w, width=conv_width)
        attn = _attention(qk, vt, bias, batch, n_heads)
        conv = _conv_module(u, conv_w, vec3(conv_b), vec3(conv_ln_g), vec3(conv_ln_b), i, batch)
        h, route_i, route_w, cnt = _outproj(attn, conv, w_out_b, xf, vec3(ln1_g), vec3(ln1_b),
                                            w_router, b_router, i, alpha)
        plan = _moe_plan(route_i, cnt, 2 * t)
        xs = _dispatch(plan[0], h)
        ys = _gmm(plan, xs, expert_w_gu, expert_w_down, i)
        xf, xb = _ple(plan[0], h, route_w, ys, vec3(ln2_g), vec3(ln2_b), w_gate_b, p2, w_ple_b, i, alpha)
    return xf.reshape(batch, seq, d)
```

```python
import functools
import math

import numpy as np
import jax
import jax.numpy as jnp
from jax import lax
from jax.experimental import pallas as pl
from jax.experimental.pallas import tpu as pltpu

F32 = jnp.float32
BF16 = jnp.bfloat16

HEAD_DIM = 128
BLOCK = 256
TOP_BLOCKS = 3
CONV_K = 31
N_BUCKETS = 32
MAX_DISTANCE = 128
N_GROUPS = 4
EXPERTS_PER_GROUP = 8
N_EXPERTS = N_GROUPS * EXPERTS_PER_GROUP
LN_EPS = 1e-5
NEG = -1e30
ROUTER_LANES = 128
VMEM_LIMIT = 56 * 1024 * 1024


def _bucket_thresholds():
    n = np.arange(0, 4 * MAX_DISTANCE)
    max_exact = N_BUCKETS // 2
    nf = np.maximum(n, max_exact).astype(np.float32)
    large = max_exact + (np.log(nf / np.float32(max_exact)) / np.float32(math.log(MAX_DISTANCE / max_exact))
                         * np.float32(N_BUCKETS - max_exact)).astype(np.int32)
    bucket = np.where(n < max_exact, n, np.minimum(large, N_BUCKETS - 1))
    assert np.all(np.diff(bucket) >= 0) and bucket[-1] == N_BUCKETS - 1
    return [int(np.argmax(bucket >= b)) for b in range(N_BUCKETS)]


_BUCKET_START = _bucket_thresholds()


def _params(**kw):
    return pltpu.CompilerParams(vmem_limit_bytes=VMEM_LIMIT, **kw)


def _bias_kernel(rb_ref, out_ref):
    h = pl.program_id(0)
    kj = lax.broadcasted_iota(jnp.int32, (BLOCK, BLOCK), 0)
    qi = lax.broadcasted_iota(jnp.int32, (BLOCK, BLOCK), 1)
    for d in range(3):
        rel = qi - kj + d * BLOCK
        val = jnp.full((BLOCK, BLOCK), rb_ref[0, h], F32)
        for b in range(1, N_BUCKETS):
            val = jnp.where(rel >= _BUCKET_START[b], rb_ref[b, h], val)
        if d == 0:
            val = jnp.where(rel >= 0, val, NEG)
        out_ref[d] = val


def _bias_tiles(rel_bias):
    n_heads = rel_bias.shape[1]
    assert _BUCKET_START[-1] <= BLOCK + 1
    return pl.pallas_call(
        _bias_kernel,
        grid=(n_heads,),
        in_specs=[pl.BlockSpec(memory_space=pltpu.SMEM)],
        out_specs=pl.BlockSpec((None, 3, BLOCK, BLOCK), lambda h: (h, 0, 0, 0)),
        out_shape=jax.ShapeDtypeStruct((n_heads, 3, BLOCK, BLOCK), F32),
        name="t5_bias_tiles",
    )(rel_bias)


def _mm_kernel(x_ref, w_ref, o_ref):
    o_ref[...] = jnp.dot(x_ref[...], w_ref[...], preferred_element_type=F32).astype(o_ref.dtype)


def _proj_qk(xb, w_in, layer, tm=512, tn=1024):
    t, d = xb.shape
    n_out = 2 * tn
    return pl.pallas_call(
        _mm_kernel,
        grid=(n_out // tn, t // tm),
        in_specs=[pl.BlockSpec((tm, d), lambda j, i: (i, 0)),
                  pl.BlockSpec((None, d, tn), lambda j, i: (layer, 0, j))],
        out_specs=pl.BlockSpec((tm, tn), lambda j, i: (i, j)),
        out_shape=jax.ShapeDtypeStruct((t, n_out), BF16),
        compiler_params=_params(),
        name="proj_qk",
    )(xb, w_in)


def _proj_vt_kernel(x_ref, w_ref, o_ref):
    r = jnp.dot(x_ref[...], w_ref[...], preferred_element_type=F32)
    o_ref[...] = r.T.astype(o_ref.dtype)


def _proj_vt(xb, w_in, layer, batch, width, col_block, tm=512):
    t, d = xb.shape
    seq = t // batch
    per_b = seq // tm
    return pl.pallas_call(
        _proj_vt_kernel,
        grid=(t // tm,),
        in_specs=[pl.BlockSpec((tm, d), lambda i: (i, 0)),
                  pl.BlockSpec((None, d, width), lambda i: (layer, 0, col_block))],
        out_specs=pl.BlockSpec((None, width, tm), lambda i: (i // per_b, 0, i % per_b)),
        out_shape=jax.ShapeDtypeStruct((batch, width, seq), BF16),
        compiler_params=_params(),
        name="proj_vt",
    )(xb, w_in)


def _proj_glu_kernel(x_ref, wa_ref, wg_ref, o_ref):
    x = x_ref[...]
    a = jnp.dot(x, wa_ref[...], preferred_element_type=F32)
    g = jnp.dot(x, wg_ref[...], preferred_element_type=F32)
    o_ref[...] = a * jax.nn.sigmoid(g)


def _proj_glu(xb, w_in, layer, a_col, width, tm=512, tn=512):
    t, d = xb.shape
    a_blk = a_col // tn
    g_blk = (a_col + width) // tn
    return pl.pallas_call(
        _proj_glu_kernel,
        grid=(width // tn, t // tm),
        in_specs=[pl.BlockSpec((tm, d), lambda j, i: (i, 0)),
                  pl.BlockSpec((None, d, tn), lambda j, i: (layer, 0, a_blk + j)),
                  pl.BlockSpec((None, d, tn), lambda j, i: (layer, 0, g_blk + j))],
        out_specs=pl.BlockSpec((tm, tn), lambda j, i: (i, j)),
        out_shape=jax.ShapeDtypeStruct((t, width), F32),
        compiler_params=_params(),
        name="proj_glu",
    )(xb, w_in, w_in)


ATTN_HEADS_PER_STEP = 2
_NT = (((1,), (1,)), ((), ()))


def _block_penalty(kmean, q, c):
    gate = lax.dot_general(kmean, q.astype(F32), _NT, precision=lax.Precision.HIGHEST,
                           preferred_element_type=F32)
    row = lax.broadcasted_iota(jnp.int32, gate.shape, 0)
    gate_m = jnp.where(row < c, gate, -jnp.inf)
    pens = []
    for n in range(c):
        g_n = gate[n:n + 1, :]
        beats = (gate_m > g_n) | ((gate_m == g_n) & (row < n))
        rank = jnp.sum(beats.astype(F32), axis=0, keepdims=True)
        pens.append(jnp.where(rank < TOP_BLOCKS, 0.0, NEG))
    return pens


def _attn_row(c, hh, q_ref, k_ref, vt_ref, bias_ref, o_ref, kmean_ref, scale):
    hs = slice(hh * HEAD_DIM, (hh + 1) * HEAD_DIM)
    q = q_ref[:, hs]
    keys = (c + 1) * BLOCK
    s = lax.dot_general(k_ref[0:keys, hs], q, _NT, preferred_element_type=F32)
    pens = _block_penalty(kmean_ref[hh], q, c) if c > TOP_BLOCKS else None
    blocks = []
    for n in range(c + 1):
        b = s[n * BLOCK:(n + 1) * BLOCK, :] * scale + bias_ref[hh, min(c - n, 2)]
        if pens is not None and n < c:
            b = b + pens[n]
        blocks.append(b)
    m = jnp.max(functools.reduce(jnp.maximum, blocks), axis=0, keepdims=True)
    ps = [jnp.exp(b - m) for b in blocks]
    l = jnp.sum(functools.reduce(jnp.add, ps), axis=0, keepdims=True)
    p = jnp.concatenate([x.astype(BF16) for x in ps], axis=0)
    acc = jnp.dot(vt_ref[hs, 0:keys], p, preferred_element_type=F32)
    o_ref[:, hs] = (acc / l).T.astype(o_ref.dtype)


def _attn_kernel(q_ref, k_ref, vt_ref, bias_ref, o_ref, kmean_ref, *, n_blocks, scale):
    c_id = pl.program_id(2)
    heads = q_ref.shape[1] // HEAD_DIM

    @pl.when(c_id == 0)
    def _():
        for hh in range(heads):
            for n in range(n_blocks):
                kb = k_ref[n * BLOCK:(n + 1) * BLOCK, hh * HEAD_DIM:(hh + 1) * HEAD_DIM]
                kmean_ref[hh, n:n + 1, :] = jnp.mean(kb.astype(F32), axis=0, keepdims=True)

    for c in range(n_blocks):
        @pl.when(c_id == c)
        def _(c=c):
            for hh in range(heads):
                _attn_row(c, hh, q_ref, k_ref, vt_ref, bias_ref, o_ref, kmean_ref, scale)


def _attention(qk, vt, bias, batch, n_heads):
    t = qk.shape[0]
    seq = t // batch
    nb = seq // BLOCK
    hp = ATTN_HEADS_PER_STEP
    w = hp * HEAD_DIM
    kern = functools.partial(_attn_kernel, n_blocks=nb, scale=HEAD_DIM ** -0.5)
    return pl.pallas_call(
        kern,
        grid=(batch, n_heads // hp, nb),
        in_specs=[pl.BlockSpec((BLOCK, w), lambda b, h, c: (b * nb + c, h)),
                  pl.BlockSpec((seq, w), lambda b, h, c: (b, n_heads // hp + h)),
                  pl.BlockSpec((None, w, seq), lambda b, h, c: (b, h, 0)),
                  pl.BlockSpec((hp, 3, BLOCK, BLOCK), lambda b, h, c: (h, 0, 0, 0))],
        out_specs=pl.BlockSpec((BLOCK, w), lambda b, h, c: (b * nb + c, h)),
        out_shape=jax.ShapeDtypeStruct((t, n_heads * HEAD_DIM), BF16),
        scratch_shapes=[pltpu.VMEM((hp, nb, HEAD_DIM), F32)],
        compiler_params=_params(),
        name="moba_attention",
    )(qk, qk, vt, bias)


CONV_TILE = 256
CONV_HALO = 32
CONV_ROWS = 64
LANES = 128
SUBLANES = 8


def _conv_kernel(prev_ref, cur_ref, w_ref, b_ref, g_ref, beta_ref, o_ref, win_ref, y_ref):
    s = pl.program_id(1)
    win_ref[0:CONV_HALO, :] = jnp.where(s > 0, prev_ref[...], 0.0)
    win_ref[CONV_HALO:, :] = cur_ref[...]
    width = cur_ref.shape[1]
    first = CONV_HALO - (CONV_K - 1)
    phases = [[k for k in range(CONV_K) if (first + k) % SUBLANES == a] for a in range(SUBLANES)]
    for lc in range(width // LANES):
        ls = slice(lc * LANES, (lc + 1) * LANES)
        for rc in range(CONV_TILE // CONV_ROWS):
            r0 = rc * CONV_ROWS
            acc = jnp.broadcast_to(b_ref[:, ls], (CONV_ROWS, LANES))
            for a, taps in enumerate(phases):
                rows = CONV_ROWS + (SUBLANES if a else 0)
                z = None
                for k in taps:
                    start = r0 + first + k - a
                    term = win_ref[start:start + rows, ls] * w_ref[k:k + 1, ls]
                    z = term if z is None else z + term
                acc = acc + z[a:a + CONV_ROWS, :]
            y_ref[r0:r0 + CONV_ROWS, ls] = acc
    y = y_ref[...]
    mu = jnp.mean(y, axis=-1, keepdims=True)
    yc = y - mu
    var = jnp.mean(yc * yc, axis=-1, keepdims=True)
    z = yc * lax.rsqrt(var + LN_EPS) * g_ref[...] + beta_ref[...]
    o_ref[...] = (z * jax.nn.sigmoid(z)).astype(o_ref.dtype)


def _conv_module(u, conv_w, conv_b, ln_g, ln_b, layer, batch):
    t, width = u.shape
    seq = t // batch
    tiles = seq // CONV_TILE
    halo_per_tile = CONV_TILE // CONV_HALO
    vec = pl.BlockSpec((None, 1, width), lambda b, s: (layer, 0, 0))
    return pl.pallas_call(
        _conv_kernel,
        grid=(batch, tiles),
        in_specs=[pl.BlockSpec((CONV_HALO, width),
                               lambda b, s: (jnp.maximum((b * tiles + s) * halo_per_tile - 1, 0), 0)),
                  pl.BlockSpec((CONV_TILE, width), lambda b, s: (b * tiles + s, 0)),
                  pl.BlockSpec((None, CONV_K, width), lambda b, s: (layer, 0, 0)),
                  vec, vec, vec],
        out_specs=pl.BlockSpec((CONV_TILE, width), lambda b, s: (b * tiles + s, 0)),
        out_shape=jax.ShapeDtypeStruct((t, width), BF16),
        scratch_shapes=[pltpu.VMEM((CONV_HALO + CONV_TILE, width), F32), pltpu.VMEM((CONV_TILE, width), F32)],
        compiler_params=_params(),
        name="conformer_conv",
    )(u, u, conv_w, conv_b, ln_g, ln_b)


def _layer_norm(y, g, b):
    mu = jnp.mean(y, axis=-1, keepdims=True)
    yc = y - mu
    var = jnp.mean(yc * yc, axis=-1, keepdims=True)
    return yc * lax.rsqrt(var + LN_EPS) * g + b


def _route(logits):
    lane = lax.broadcasted_iota(jnp.int32, logits.shape, 1)
    is_group = lane < N_GROUPS
    gl = jnp.where(is_group, logits, -jnp.inf)
    gmax = jnp.max(gl, axis=-1, keepdims=True)
    g_top = jnp.min(jnp.where(gl == gmax, lane, ROUTER_LANES), axis=-1, keepdims=True)
    p_group = 1.0 / jnp.sum(jnp.where(is_group, jnp.exp(gl - gmax), 0.0), axis=-1, keepdims=True)
    e_idx = lane - N_GROUPS
    in_group = (e_idx >= 0) & (e_idx < N_EXPERTS) & ((e_idx // EXPERTS_PER_GROUP) == g_top)
    el = jnp.where(in_group, logits, -jnp.inf)
    m1 = jnp.max(el, axis=-1, keepdims=True)
    i1 = jnp.min(jnp.where(el == m1, lane, ROUTER_LANES), axis=-1, keepdims=True)
    el2 = jnp.where(lane == i1, -jnp.inf, el)
    m2 = jnp.max(el2, axis=-1, keepdims=True)
    i2 = jnp.min(jnp.where(el2 == m2, lane, ROUTER_LANES), axis=-1, keepdims=True)
    e2 = jnp.exp(m2 - m1)
    w1 = p_group / (1.0 + e2)
    w2 = p_group * e2 / (1.0 + e2)
    return i1, i2, w1, w2


def _outproj_kernel(attn_ref, conv_ref, w_ref, x_ref, g_ref, b_ref, wr_ref, br_ref,
                    h_ref, ri_ref, rw_ref, cnt_ref, carry_ref, *, alpha):
    i = pl.program_id(0)

    @pl.when(i == 0)
    def _():
        carry_ref[...] = jnp.zeros_like(carry_ref)

    ka = attn_ref.shape[1]
    mix = jnp.dot(attn_ref[...], w_ref[0:ka, :], preferred_element_type=F32)
    mix = mix + jnp.dot(conv_ref[...], w_ref[ka:, :], preferred_element_type=F32)
    h = _layer_norm(alpha * x_ref[...] + mix, g_ref[...], b_ref[...])
    h_ref[...] = h
    h_hi = h.astype(BF16)
    h_lo = (h - h_hi.astype(F32)).astype(BF16)
    t_hi = jnp.dot(h_hi, wr_ref[...], preferred_element_type=F32)
    t_lo = jnp.dot(h_lo, wr_ref[:, 0:ROUTER_LANES], preferred_element_type=F32)
    logits = t_hi[:, 0:ROUTER_LANES] + t_hi[:, ROUTER_LANES:] + t_lo + br_ref[...]
    i1, i2, w1, w2 = _route(logits)

    tm = logits.shape[0]
    lane = lax.broadcasted_iota(jnp.int32, logits.shape, 1)
    onehot = jnp.where((lane == i1) | (lane == i2), 1.0, 0.0)
    rr = lax.broadcasted_iota(jnp.int32, (tm, tm), 0)
    cc = lax.broadcasted_iota(jnp.int32, (tm, tm), 1)
    earlier = jnp.where(cc < rr, 1.0, 0.0).astype(BF16)
    before = jnp.dot(earlier, onehot.astype(BF16), preferred_element_type=F32) + carry_ref[...]
    r1 = jnp.sum(jnp.where(lane == i1, before, 0.0), axis=-1, keepdims=True)
    r2 = jnp.sum(jnp.where(lane == i2, before, 0.0), axis=-1, keepdims=True)
    carry_ref[...] += jnp.sum(onehot, axis=0, keepdims=True)
    cnt_ref[...] = jnp.broadcast_to(carry_ref[...], cnt_ref.shape)
    rec = jnp.where(lane == 0, (i1 - N_GROUPS).astype(F32),
                    jnp.where(lane == 1, (i2 - N_GROUPS).astype(F32),
                              jnp.where(lane == 2, r1, jnp.where(lane == 3, r2, 0.0))))
    ri_ref[...] = rec.T[0:ri_ref.shape[0], :].astype(jnp.int32)
    rw_ref[...] = jnp.where(lane == 0, w1, jnp.where(lane == 1, w2, 0.0))


def _outproj(attn, conv, w_out, x, ln_g, ln_b, w_router, b_router, layer, alpha, tm=256):
    t, d = x.shape
    ka, kc = attn.shape[1], conv.shape[1]
    vec = pl.BlockSpec((None, 1, d), lambda i: (layer, 0, 0))
    row = lambda w: pl.BlockSpec((tm, w), lambda i: (i, 0))
    return pl.pallas_call(
        functools.partial(_outproj_kernel, alpha=alpha),
        grid=(t // tm,),
        in_specs=[row(ka), row(kc),
                  pl.BlockSpec((None, ka + kc, d), lambda i: (layer, 0, 0)),
                  row(d), vec, vec,
                  pl.BlockSpec((None, d, 2 * ROUTER_LANES), lambda i: (layer, 0, 0)),
                  pl.BlockSpec((None, 1, ROUTER_LANES), lambda i: (layer, 0, 0))],
        out_specs=[row(d), pl.BlockSpec((SUBLANES, tm), lambda i: (0, i)), row(ROUTER_LANES),
                   pl.BlockSpec((8, ROUTER_LANES), lambda i: (0, 0))],
        out_shape=[jax.ShapeDtypeStruct((t, d), F32),
                   jax.ShapeDtypeStruct((SUBLANES, t), jnp.int32),
                   jax.ShapeDtypeStruct((t, ROUTER_LANES), F32),
                   jax.ShapeDtypeStruct((8, ROUTER_LANES), F32)],
        scratch_shapes=[pltpu.VMEM((1, ROUTER_LANES), F32)],
        compiler_params=_params(dimension_semantics=("arbitrary",)),
        name="outproj_ln_router",
    )(attn, conv, w_out, x, ln_g, ln_b, w_router, b_router)


MOE_TILE = 256


def _moe_plan(route_i, cnt, n_rows):
    e = route_i[0:2, :]
    rank = route_i[2:4, :]
    counts = cnt[0, N_GROUPS:N_GROUPS + N_EXPERTS].astype(jnp.int32)
    ends = jnp.cumsum(counts)
    offs = ends - counts
    start = functools.reduce(jnp.add, [jnp.where(e == k, offs[k], 0) for k in range(N_EXPERTS)])
    pos = (start + rank).reshape(-1)
    n_tiles = n_rows // MOE_TILE
    first_tile = offs // MOE_TILE
    n_items_e = jnp.where(counts > 0, (ends - 1) // MOE_TILE - first_tile + 1, 0)
    item_end = jnp.cumsum(n_items_e)
    item_start = item_end - n_items_e
    n_items = n_tiles + N_EXPERTS - 1
    j = jnp.arange(n_items, dtype=jnp.int32)
    valid = j < item_end[-1]
    jj = jnp.clip(j, 0, jnp.maximum(item_end[-1] - 1, 0))
    ex = jnp.minimum(jnp.sum((jj[:, None] >= item_end[None, :]).astype(jnp.int32), axis=1), N_EXPERTS - 1)
    tile = first_tile[ex] + (jj - item_start[ex])
    lo = jnp.where(valid, jnp.maximum(offs[ex], tile * MOE_TILE) - tile * MOE_TILE, 0)
    hi = jnp.where(valid, jnp.minimum(ends[ex], (tile + 1) * MOE_TILE) - tile * MOE_TILE, 0)
    return pos, tile.astype(jnp.int32), ex.astype(jnp.int32), lo.astype(jnp.int32), hi.astype(jnp.int32)


def _dispatch_kernel(pos_ref, h_ref, xs_ref, sem):
    i = pl.program_id(0)
    tm = h_ref.shape[0]
    tokens = pos_ref.shape[0] // 2
    copies = []
    for r in range(tm):
        for s in range(2):
            dst = pos_ref[s * tokens + i * tm + r]
            cp = pltpu.make_async_copy(h_ref.at[pl.ds(r, 1), :], xs_ref.at[pl.ds(dst, 1), :], sem)
            cp.start()
            copies.append(cp)
    for cp in copies:
        cp.wait()


def _dispatch(pos, h, tm=256):
    t, d = h.shape
    return pl.pallas_call(
        _dispatch_kernel,
        grid_spec=pltpu.PrefetchScalarGridSpec(
            num_scalar_prefetch=1,
            grid=(t // tm,),
            in_specs=[pl.BlockSpec((tm, d), lambda i, pos: (i, 0))],
            out_specs=pl.BlockSpec(memory_space=pl.ANY),
            scratch_shapes=[pltpu.SemaphoreType.DMA]),
        out_shape=jax.ShapeDtypeStruct((2 * t, d), F32),
        compiler_params=_params(),
        name="moe_dispatch",
    )(pos, h)


def _gmm_kernel(tile_ref, exp_ref, lo_ref, hi_ref, xs_ref, wgu_ref, wd_ref, ys_ref):
    j = pl.program_id(0)
    f = wd_ref.shape[0]
    hg = jnp.dot(xs_ref[...].astype(BF16), wgu_ref[...].astype(BF16), preferred_element_type=F32)
    gate = hg[:, :f]
    row = lax.broadcasted_iota(jnp.int32, gate.shape, 0)
    mine = (row >= lo_ref[j]) & (row < hi_ref[j])
    act = jnp.where(mine, gate * jax.nn.sigmoid(gate) * hg[:, f:], 0.0)
    y = jnp.dot(act.astype(BF16), wd_ref[...].astype(BF16), preferred_element_type=F32)
    first = (j == 0) | (tile_ref[j] != tile_ref[jnp.maximum(j - 1, 0)])

    @pl.when(first)
    def _():
        ys_ref[...] = y

    @pl.when(jnp.logical_not(first))
    def _():
        ys_ref[...] += y


def _gmm(plan, xs, w_gu, w_down, layer):
    _, tile, ex, lo, hi = plan
    n_rows, d = xs.shape
    f2 = w_gu.shape[-1]
    f = w_down.shape[2]
    return pl.pallas_call(
        _gmm_kernel,
        grid_spec=pltpu.PrefetchScalarGridSpec(
            num_scalar_prefetch=4,
            grid=(tile.shape[0],),
            in_specs=[pl.BlockSpec((MOE_TILE, d), lambda j, tile, ex, lo, hi: (tile[j], 0)),
                      pl.BlockSpec((None, None, d, f2), lambda j, tile, ex, lo, hi: (layer, ex[j], 0, 0)),
                      pl.BlockSpec((None, None, f, d), lambda j, tile, ex, lo, hi: (layer, ex[j], 0, 0))],
            out_specs=pl.BlockSpec((MOE_TILE, d), lambda j, tile, ex, lo, hi: (tile[j], 0))),
        out_shape=jax.ShapeDtypeStruct((n_rows, d), F32),
        compiler_params=_params(dimension_semantics=("arbitrary",)),
        name="moe_grouped_matmul",
    )(tile, ex, lo, hi, xs, w_gu, w_down)


def _ple_kernel(pos_ref, h_ref, rw_ref, ys_ref, g_ref, b_ref, wgate_ref, p_ref, wp_ref, x_ref, xb_ref,
                ybuf, sems, *, alpha):
    i = pl.program_id(0)
    n_tiles = pl.num_programs(0) - 1
    tm = h_ref.shape[0]
    tokens = pos_ref.shape[0] // 2

    def row_copies(tile):
        slot = tile % 2
        return [pltpu.make_async_copy(ys_ref.at[pl.ds(pos_ref[s * tokens + tile * tm + r], 1), :],
                                      ybuf.at[slot, s, pl.ds(r, 1), :], sems.at[slot])
                for r in range(tm) for s in range(2)]

    @pl.when(i < n_tiles)
    def _():
        for cp in row_copies(i):
            cp.start()

    @pl.when(i > 0)
    def _():
        for cp in row_copies(i - 1):
            cp.wait()
        slot = (i - 1) % 2
        rw = rw_ref[...]
        f = rw[:, 0:1] * ybuf[slot, 0] + rw[:, 1:2] * ybuf[slot, 1]
        h2 = _layer_norm(alpha * h_ref[...] + f, g_ref[...], b_ref[...])
        gate = jax.nn.sigmoid(jnp.dot(h2.astype(BF16), wgate_ref[...], preferred_element_type=F32))
        pe = jnp.dot(p_ref[...].astype(BF16), wp_ref[...], preferred_element_type=F32)
        x_new = h2 + gate * pe
        x_ref[...] = x_new
        xb_ref[...] = x_new.astype(BF16)


def _ple(pos, h, rw, ys, ln_g, ln_b, w_gate, p, w_ple, layer, alpha, tm=256):
    t, d = h.shape
    pd = p.shape[-1]
    prev = lambda i: jnp.maximum(i - 1, 0)
    vec = pl.BlockSpec((None, 1, d), lambda i, pos: (layer, 0, 0))
    row = lambda w: pl.BlockSpec((tm, w), lambda i, pos: (prev(i), 0))
    return pl.pallas_call(
        functools.partial(_ple_kernel, alpha=alpha),
        grid_spec=pltpu.PrefetchScalarGridSpec(
            num_scalar_prefetch=1,
            grid=(t // tm + 1,),
            in_specs=[row(d), row(ROUTER_LANES), pl.BlockSpec(memory_space=pl.ANY), vec, vec,
                      pl.BlockSpec((None, d, d), lambda i, pos: (layer, 0, 0)),
                      pl.BlockSpec((None, tm, pd), lambda i, pos: (layer, prev(i), 0)),
                      pl.BlockSpec((None, pd, d), lambda i, pos: (layer, 0, 0))],
            out_specs=[row(d), row(d)],
            scratch_shapes=[pltpu.VMEM((2, 2, tm, d), F32), pltpu.SemaphoreType.DMA((2,))]),
        out_shape=[jax.ShapeDtypeStruct((t, d), F32), jax.ShapeDtypeStruct((t, d), BF16)],
        compiler_params=_params(dimension_semantics=("arbitrary",)),
        name="gather_ln2_ple",
    )(pos, h, rw, ys, ln_g, ln_b, w_gate, p, w_ple)


def kernel(x, p, w_in, conv_w, conv_b, conv_ln_g, conv_ln_b, w_out, rel_bias, ln1_g, ln1_b, router_g_w, router_g_b, router_e_w, router_e_b, expert_w_gu, expert_w_down, ln2_g, ln2_b, ple_w, ple_gate_w):
    batch, seq, d = x.shape
    depth = w_in.shape[0]
    n_heads = rel_bias.shape[1]
    attn_w = n_heads * HEAD_DIM
    conv_width = conv_w.shape[2]
    t = batch * seq
    alpha = (2 * depth) ** 0.25

    w_in_b = w_in.astype(BF16)
    w_out_b = w_out.astype(BF16)
    w_gate_b = ple_gate_w.astype(BF16)
    w_ple_b = ple_w.astype(BF16)
    pad = ROUTER_LANES - N_GROUPS - N_EXPERTS
    w_router = jnp.pad(jnp.concatenate([router_g_w, router_e_w], axis=-1), ((0, 0), (0, 0), (0, pad)))
    w_router_hi = w_router.astype(BF16)
    w_router_lo = (w_router - w_router_hi.astype(F32)).astype(BF16)
    w_router = jnp.concatenate([w_router_hi, w_router_lo], axis=-1)
    b_router = jnp.pad(jnp.concatenate([router_g_b, router_e_b], axis=-1), ((0, 0), (0, pad)))[:, None, :]
    vec3 = lambda v: v[:, None, :]
    p2 = p.reshape(depth, t, p.shape[-1])

    bias = _bias_tiles(rel_bias)
    xf = x.reshape(t, d)
    xb = xf.astype(BF16)
    for i in range(depth):
        qk = _proj_qk(xb, w_in_b, i, tn=attn_w)
        vt = _proj_vt(xb, w_in_b, i, batch, attn_w, col_block=2)
        u = _proj_glu(xb, w_in_b, i, a_col=3 * attn_w, width=conv_width)
        attn = _attention(qk, vt, bias, batch, n_heads)
        conv = _conv_module(u, conv_w, vec3(conv_b), vec3(conv_ln_g), vec3(conv_ln_b), i, batch)
        h, route_i, route_w, cnt = _outproj(attn, conv, w_out_b, xf, vec3(ln1_g), vec3(ln1_b),
                                            w_router, b_router, i, alpha)
        plan = _moe_plan(route_i, cnt, 2 * t)
        xs = _dispatch(plan[0], h)
        ys = _gmm(plan, xs, expert_w_gu, expert_w_down, i)
        xf, xb = _ple(plan[0], h, route_w, ys, vec3(ln2_g), vec3(ln2_b), w_gate_b, p2, w_ple_b, i, alpha)
    return xf.reshape(batch, seq, d)
```

```python
import functools
import math

import numpy as np
import jax
import jax.numpy as jnp
from jax import lax
from jax.experimental import pallas as pl
from jax.experimental.pallas import tpu as pltpu

F32 = jnp.float32
BF16 = jnp.bfloat16

HEAD_DIM = 128
BLOCK = 256
TOP_BLOCKS = 3
CONV_K = 31
N_BUCKETS = 32
MAX_DISTANCE = 128
N_GROUPS = 4
EXPERTS_PER_GROUP = 8
N_EXPERTS = N_GROUPS * EXPERTS_PER_GROUP
LN_EPS = 1e-5
NEG = -1e30
ROUTER_LANES = 128
VMEM_LIMIT = 56 * 1024 * 1024


def _bucket_thresholds():
    n = np.arange(0, 4 * MAX_DISTANCE)
    max_exact = N_BUCKETS // 2
    nf = np.maximum(n, max_exact).astype(np.float32)
    large = max_exact + (np.log(nf / np.float32(max_exact)) / np.float32(math.log(MAX_DISTANCE / max_exact))
                         * np.float32(N_BUCKETS - max_exact)).astype(np.int32)
    bucket = np.where(n < max_exact, n, np.minimum(large, N_BUCKETS - 1))
    assert np.all(np.diff(bucket) >= 0) and bucket[-1] == N_BUCKETS - 1
    return [int(np.argmax(bucket >= b)) for b in range(N_BUCKETS)]


_BUCKET_START = _bucket_thresholds()


def _params(**kw):
    return pltpu.CompilerParams(vmem_limit_bytes=VMEM_LIMIT, **kw)


def _bias_kernel(rb_ref, out_ref):
    h = pl.program_id(0)
    kj = lax.broadcasted_iota(jnp.int32, (BLOCK, BLOCK), 0)
    qi = lax.broadcasted_iota(jnp.int32, (BLOCK, BLOCK), 1)
    for d in range(3):
        rel = qi - kj + d * BLOCK
        val = jnp.full((BLOCK, BLOCK), rb_ref[0, h], F32)
        for b in range(1, N_BUCKETS):
            val = jnp.where(rel >= _BUCKET_START[b], rb_ref[b, h], val)
        if d == 0:
            val = jnp.where(rel >= 0, val, NEG)
        out_ref[d] = val


def _bias_tiles(rel_bias):
    n_heads = rel_bias.shape[1]
    assert _BUCKET_START[-1] <= BLOCK + 1
    return pl.pallas_call(
        _bias_kernel,
        grid=(n_heads,),
        in_specs=[pl.BlockSpec(memory_space=pltpu.SMEM)],
        out_specs=pl.BlockSpec((None, 3, BLOCK, BLOCK), lambda h: (h, 0, 0, 0)),
        out_shape=jax.ShapeDtypeStruct((n_heads, 3, BLOCK, BLOCK), F32),
        name="t5_bias_tiles",
    )(rel_bias)


def _cast_weights_once(row_tile_id, pairs):
    @pl.when(row_tile_id == 0)
    def _():
        for src, dst in pairs:
            dst[...] = src[...].astype(BF16)


def _mm_kernel(x_ref, w_ref, o_ref, wb_ref):
    _cast_weights_once(pl.program_id(1), [(w_ref, wb_ref)])
    o_ref[...] = jnp.dot(x_ref[...], wb_ref[...], preferred_element_type=F32).astype(o_ref.dtype)


def _proj_qk(xb, w_in, layer, tm=512, tn=1024):
    t, d = xb.shape
    n_out = 2 * tn
    return pl.pallas_call(
        _mm_kernel,
        grid=(n_out // tn, t // tm),
        in_specs=[pl.BlockSpec((tm, d), lambda j, i: (i, 0)),
                  pl.BlockSpec((None, d, tn), lambda j, i: (layer, 0, j))],
        out_specs=pl.BlockSpec((tm, tn), lambda j, i: (i, j)),
        out_shape=jax.ShapeDtypeStruct((t, n_out), BF16),
        scratch_shapes=[pltpu.VMEM((d, tn), BF16)],
        compiler_params=_params(dimension_semantics=("arbitrary", "arbitrary")),
        name="proj_qk",
    )(xb, w_in)


def _proj_vt_kernel(x_ref, w_ref, o_ref, wb_ref):
    _cast_weights_once(pl.program_id(0), [(w_ref, wb_ref)])
    r = jnp.dot(x_ref[...], wb_ref[...], preferred_element_type=F32)
    o_ref[...] = r.T.astype(o_ref.dtype)


def _proj_vt(xb, w_in, layer, batch, width, col_block, tm=512):
    t, d = xb.shape
    seq = t // batch
    per_b = seq // tm
    return pl.pallas_call(
        _proj_vt_kernel,
        grid=(t // tm,),
        in_specs=[pl.BlockSpec((tm, d), lambda i: (i, 0)),
                  pl.BlockSpec((None, d, width), lambda i: (layer, 0, col_block))],
        out_specs=pl.BlockSpec((None, width, tm), lambda i: (i // per_b, 0, i % per_b)),
        out_shape=jax.ShapeDtypeStruct((batch, width, seq), BF16),
        scratch_shapes=[pltpu.VMEM((d, width), BF16)],
        compiler_params=_params(dimension_semantics=("arbitrary",)),
        name="proj_vt",
    )(xb, w_in)


def _proj_glu_kernel(x_ref, wa_ref, wg_ref, o_ref, wab_ref, wgb_ref):
    _cast_weights_once(pl.program_id(1), [(wa_ref, wab_ref), (wg_ref, wgb_ref)])
    x = x_ref[...]
    a = jnp.dot(x, wab_ref[...], preferred_element_type=F32)
    g = jnp.dot(x, wgb_ref[...], preferred_element_type=F32)
    o_ref[...] = a * jax.nn.sigmoid(g)


def _proj_glu(xb, w_in, layer, a_col, width, tm=512, tn=512):
    t, d = xb.shape
    a_blk = a_col // tn
    g_blk = (a_col + width) // tn
    return pl.pallas_call(
        _proj_glu_kernel,
        grid=(width // tn, t // tm),
        in_specs=[pl.BlockSpec((tm, d), lambda j, i: (i, 0)),
                  pl.BlockSpec((None, d, tn), lambda j, i: (layer, 0, a_blk + j)),
                  pl.BlockSpec((None, d, tn), lambda j, i: (layer, 0, g_blk + j))],
        out_specs=pl.BlockSpec((tm, tn), lambda j, i: (i, j)),
        out_shape=jax.ShapeDtypeStruct((t, width), F32),
        scratch_shapes=[pltpu.VMEM((d, tn), BF16), pltpu.VMEM((d, tn), BF16)],
        compiler_params=_params(dimension_semantics=("arbitrary", "arbitrary")),
        name="proj_glu",
    )(xb, w_in, w_in)


ATTN_HEADS_PER_STEP = 2
_NT = (((1,), (1,)), ((), ()))


def _block_penalty(kmean, q, c):
    gate = lax.dot_general(kmean, q.astype(F32), _NT, precision=lax.Precision.HIGHEST,
                           preferred_element_type=F32)
    row = lax.broadcasted_iota(jnp.int32, gate.shape, 0)
    gate_m = jnp.where(row < c, gate, -jnp.inf)
    pens = []
    for n in range(c):
        g_n = gate[n:n + 1, :]
        beats = (gate_m > g_n) | ((gate_m == g_n) & (row < n))
        rank = jnp.sum(beats.astype(F32), axis=0, keepdims=True)
        pens.append(jnp.where(rank < TOP_BLOCKS, 0.0, NEG))
    return pens


def _attn_row(c, hh, q_ref, k_ref, vt_ref, bias_ref, o_ref, kmean_ref, scale):
    hs = slice(hh * HEAD_DIM, (hh + 1) * HEAD_DIM)
    q = q_ref[:, hs]
    keys = (c + 1) * BLOCK
    s = lax.dot_general(k_ref[0:keys, hs], q, _NT, preferred_element_type=F32)
    pens = _block_penalty(kmean_ref[hh], q, c) if c > TOP_BLOCKS else None
    blocks = []
    for n in range(c + 1):
        b = s[n * BLOCK:(n + 1) * BLOCK, :] * scale + bias_ref[hh, min(c - n, 2)]
        if pens is not None and n < c:
            b = b + pens[n]
        blocks.append(b)
    m = jnp.max(functools.reduce(jnp.maximum, blocks), axis=0, keepdims=True)
    ps = [jnp.exp(b - m) for b in blocks]
    l = jnp.sum(functools.reduce(jnp.add, ps), axis=0, keepdims=True)
    p = jnp.concatenate([x.astype(BF16) for x in ps], axis=0)
    acc = jnp.dot(vt_ref[hs, 0:keys], p, preferred_element_type=F32)
    o_ref[:, hs] = (acc / l).T.astype(o_ref.dtype)


def _attn_kernel(q_ref, k_ref, vt_ref, bias_ref, o_ref, kmean_ref, *, n_blocks, scale):
    c_id = pl.program_id(2)
    heads = q_ref.shape[1] // HEAD_DIM

    @pl.when(c_id == 0)
    def _():
        for hh in range(heads):
            for n in range(n_blocks):
                kb = k_ref[n * BLOCK:(n + 1) * BLOCK, hh * HEAD_DIM:(hh + 1) * HEAD_DIM]
                kmean_ref[hh, n:n + 1, :] = jnp.mean(kb.astype(F32), axis=0, keepdims=True)

    for c in range(n_blocks):
        @pl.when(c_id == c)
        def _(c=c):
            for hh in range(heads):
                _attn_row(c, hh, q_ref, k_ref, vt_ref, bias_ref, o_ref, kmean_ref, scale)


def _attention(qk, vt, bias, batch, n_heads):
    t = qk.shape[0]
    seq = t // batch
    nb = seq // BLOCK
    hp = ATTN_HEADS_PER_STEP
    w = hp * HEAD_DIM
    kern = functools.partial(_attn_kernel, n_blocks=nb, scale=HEAD_DIM ** -0.5)
    return pl.pallas_call(
        kern,
        grid=(batch, n_heads // hp, nb),
        in_specs=[pl.BlockSpec((BLOCK, w), lambda b, h, c: (b * nb + c, h)),
                  pl.BlockSpec((seq, w), lambda b, h, c: (b, n_heads // hp + h)),
                  pl.BlockSpec((None, w, seq), lambda b, h, c: (b, h, 0)),
                  pl.BlockSpec((hp, 3, BLOCK, BLOCK), lambda b, h, c: (h, 0, 0, 0))],
        out_specs=pl.BlockSpec((BLOCK, w), lambda b, h, c: (b * nb + c, h)),
        out_shape=jax.ShapeDtypeStruct((t, n_heads * HEAD_DIM), BF16),
        scratch_shapes=[pltpu.VMEM((hp, nb, HEAD_DIM), F32)],
        compiler_params=_params(),
        name="moba_attention",
    )(qk, qk, vt, bias)


CONV_TILE = 256
CONV_HALO = 32
CONV_ROWS = 64
LANES = 128
SUBLANES = 8


def _conv_kernel(prev_ref, cur_ref, w_ref, b_ref, g_ref, beta_ref, o_ref, win_ref, y_ref):
    s = pl.program_id(1)
    win_ref[0:CONV_HALO, :] = jnp.where(s > 0, prev_ref[...], 0.0)
    win_ref[CONV_HALO:, :] = cur_ref[...]
    width = cur_ref.shape[1]
    first = CONV_HALO - (CONV_K - 1)
    phases = [[k for k in range(CONV_K) if (first + k) % SUBLANES == a] for a in range(SUBLANES)]
    for lc in range(width // LANES):
        ls = slice(lc * LANES, (lc + 1) * LANES)
        for rc in range(CONV_TILE // CONV_ROWS):
            r0 = rc * CONV_ROWS
            acc = jnp.broadcast_to(b_ref[:, ls], (CONV_ROWS, LANES))
            for a, taps in enumerate(phases):
                rows = CONV_ROWS + (SUBLANES if a else 0)
                z = None
                for k in taps:
                    start = r0 + first + k - a
                    term = win_ref[start:start + rows, ls] * w_ref[k:k + 1, ls]
                    z = term if z is None else z + term
                acc = acc + z[a:a + CONV_ROWS, :]
            y_ref[r0:r0 + CONV_ROWS, ls] = acc
    y = y_ref[...]
    mu = jnp.mean(y, axis=-1, keepdims=True)
    yc = y - mu
    var = jnp.mean(yc * yc, axis=-1, keepdims=True)
    z = yc * lax.rsqrt(var + LN_EPS) * g_ref[...] + beta_ref[...]
    o_ref[...] = (z * jax.nn.sigmoid(z)).astype(o_ref.dtype)


def _conv_module(u, conv_w, conv_b, ln_g, ln_b, layer, batch):
    t, width = u.shape
    seq = t // batch
    tiles = seq // CONV_TILE
    halo_per_tile = CONV_TILE // CONV_HALO
    vec = pl.BlockSpec((None, 1, width), lambda b, s: (layer, 0, 0))
    return pl.pallas_call(
        _conv_kernel,
        grid=(batch, tiles),
        in_specs=[pl.BlockSpec((CONV_HALO, width),
                               lambda b, s: (jnp.maximum((b * tiles + s) * halo_per_tile - 1, 0), 0)),
                  pl.BlockSpec((CONV_TILE, width), lambda b, s: (b * tiles + s, 0)),
                  pl.BlockSpec((None, CONV_K, width), lambda b, s: (layer, 0, 0)),
                  vec, vec, vec],
        out_specs=pl.BlockSpec((CONV_TILE, width), lambda b, s: (b * tiles + s, 0)),
        out_shape=jax.ShapeDtypeStruct((t, width), BF16),
        scratch_shapes=[pltpu.VMEM((CONV_HALO + CONV_TILE, width), F32), pltpu.VMEM((CONV_TILE, width), F32)],
        compiler_params=_params(),
        name="conformer_conv",
    )(u, u, conv_w, conv_b, ln_g, ln_b)


def _layer_norm(y, g, b):
    mu = jnp.mean(y, axis=-1, keepdims=True)
    yc = y - mu
    var = jnp.mean(yc * yc, axis=-1, keepdims=True)
    return yc * lax.rsqrt(var + LN_EPS) * g + b


def _route(logits):
    lane = lax.broadcasted_iota(jnp.int32, logits.shape, 1)
    is_group = lane < N_GROUPS
    gl = jnp.where(is_group, logits, -jnp.inf)
    gmax = jnp.max(gl, axis=-1, keepdims=True)
    g_top = jnp.min(jnp.where(gl == gmax, lane, ROUTER_LANES), axis=-1, keepdims=True)
    p_group = 1.0 / jnp.sum(jnp.where(is_group, jnp.exp(gl - gmax), 0.0), axis=-1, keepdims=True)
    e_idx = lane - N_GROUPS
    in_group = (e_idx >= 0) & (e_idx < N_EXPERTS) & ((e_idx // EXPERTS_PER_GROUP) == g_top)
    el = jnp.where(in_group, logits, -jnp.inf)
    m1 = jnp.max(el, axis=-1, keepdims=True)
    i1 = jnp.min(jnp.where(el == m1, lane, ROUTER_LANES), axis=-1, keepdims=True)
    el2 = jnp.where(lane == i1, -jnp.inf, el)
    m2 = jnp.max(el2, axis=-1, keepdims=True)
    i2 = jnp.min(jnp.where(el2 == m2, lane, ROUTER_LANES), axis=-1, keepdims=True)
    e2 = jnp.exp(m2 - m1)
    w1 = p_group / (1.0 + e2)
    w2 = p_group * e2 / (1.0 + e2)
    return i1, i2, w1, w2


def _outproj_kernel(attn_ref, conv_ref, w_ref, x_ref, g_ref, b_ref, wr_ref, br_ref,
                    h_ref, ri_ref, rw_ref, cnt_ref, carry_ref, wb_ref, *, alpha):
    i = pl.program_id(0)
    _cast_weights_once(i, [(w_ref, wb_ref)])

    @pl.when(i == 0)
    def _():
        carry_ref[...] = jnp.zeros_like(carry_ref)

    ka = attn_ref.shape[1]
    mix = jnp.dot(attn_ref[...], wb_ref[0:ka, :], preferred_element_type=F32)
    mix = mix + jnp.dot(conv_ref[...], wb_ref[ka:, :], preferred_element_type=F32)
    h = _layer_norm(alpha * x_ref[...] + mix, g_ref[...], b_ref[...])
    h_ref[...] = h
    h_hi = h.astype(BF16)
    h_lo = (h - h_hi.astype(F32)).astype(BF16)
    t_hi = jnp.dot(h_hi, wr_ref[...], preferred_element_type=F32)
    t_lo = jnp.dot(h_lo, wr_ref[:, 0:ROUTER_LANES], preferred_element_type=F32)
    logits = t_hi[:, 0:ROUTER_LANES] + t_hi[:, ROUTER_LANES:] + t_lo + br_ref[...]
    i1, i2, w1, w2 = _route(logits)

    tm = logits.shape[0]
    lane = lax.broadcasted_iota(jnp.int32, logits.shape, 1)
    onehot = jnp.where((lane == i1) | (lane == i2), 1.0, 0.0)
    rr = lax.broadcasted_iota(jnp.int32, (tm, tm), 0)
    cc = lax.broadcasted_iota(jnp.int32, (tm, tm), 1)
    earlier = jnp.where(cc < rr, 1.0, 0.0).astype(BF16)
    before = jnp.dot(earlier, onehot.astype(BF16), preferred_element_type=F32) + carry_ref[...]
    r1 = jnp.sum(jnp.where(lane == i1, before, 0.0), axis=-1, keepdims=True)
    r2 = jnp.sum(jnp.where(lane == i2, before, 0.0), axis=-1, keepdims=True)
    carry_ref[...] += jnp.sum(onehot, axis=0, keepdims=True)
    cnt_ref[...] = jnp.broadcast_to(carry_ref[...], cnt_ref.shape)
    rec = jnp.where(lane == 0, (i1 - N_GROUPS).astype(F32),
                    jnp.where(lane == 1, (i2 - N_GROUPS).astype(F32),
                              jnp.where(lane == 2, r1, jnp.where(lane == 3, r2, 0.0))))
    ri_ref[...] = rec.T[0:ri_ref.shape[0], :].astype(jnp.int32)
    rw_ref[...] = jnp.where(lane == 0, w1, jnp.where(lane == 1, w2, 0.0))


def _outproj(attn, conv, w_out, x, ln_g, ln_b, w_router, b_router, layer, alpha, tm=256):
    t, d = x.shape
    ka, kc = attn.shape[1], conv.shape[1]
    vec = pl.BlockSpec((None, 1, d), lambda i: (layer, 0, 0))
    row = lambda w: pl.BlockSpec((tm, w), lambda i: (i, 0))
    return pl.pallas_call(
        functools.partial(_outproj_kernel, alpha=alpha),
        grid=(t // tm,),
        in_specs=[row(ka), row(kc),
                  pl.BlockSpec((None, ka + kc, d), lambda i: (layer, 0, 0), pipeline_mode=pl.Buffered(1)),
                  row(d), vec, vec,
                  pl.BlockSpec((None, d, 2 * ROUTER_LANES), lambda i: (layer, 0, 0)),
                  pl.BlockSpec((None, 1, ROUTER_LANES), lambda i: (layer, 0, 0))],
        out_specs=[row(d), pl.BlockSpec((SUBLANES, tm), lambda i: (0, i)), row(ROUTER_LANES),
                   pl.BlockSpec((8, ROUTER_LANES), lambda i: (0, 0))],
        out_shape=[jax.ShapeDtypeStruct((t, d), F32),
                   jax.ShapeDtypeStruct((SUBLANES, t), jnp.int32),
                   jax.ShapeDtypeStruct((t, ROUTER_LANES), F32),
                   jax.ShapeDtypeStruct((8, ROUTER_LANES), F32)],
        scratch_shapes=[pltpu.VMEM((1, ROUTER_LANES), F32), pltpu.VMEM((ka + kc, d), BF16)],
        compiler_params=_params(dimension_semantics=("arbitrary",)),
        name="outproj_ln_router",
    )(attn, conv, w_out, x, ln_g, ln_b, w_router, b_router)


MOE_TILE = 256


def _moe_plan(route_i, cnt, n_rows):
    e = route_i[0:2, :]
    rank = route_i[2:4, :]
    counts = cnt[0, N_GROUPS:N_GROUPS + N_EXPERTS].astype(jnp.int32)
    ends = jnp.cumsum(counts)
    offs = ends - counts
    start = functools.reduce(jnp.add, [jnp.where(e == k, offs[k], 0) for k in range(N_EXPERTS)])
    pos = (start + rank).reshape(-1)
    n_tiles = n_rows // MOE_TILE
    first_tile = offs // MOE_TILE
    n_items_e = jnp.where(counts > 0, (ends - 1) // MOE_TILE - first_tile + 1, 0)
    item_end = jnp.cumsum(n_items_e)
    item_start = item_end - n_items_e
    n_items = n_tiles + N_EXPERTS - 1
    j = jnp.arange(n_items, dtype=jnp.int32)
    valid = j < item_end[-1]
    jj = jnp.clip(j, 0, jnp.maximum(item_end[-1] - 1, 0))
    ex = jnp.minimum(jnp.sum((jj[:, None] >= item_end[None, :]).astype(jnp.int32), axis=1), N_EXPERTS - 1)
    tile = first_tile[ex] + (jj - item_start[ex])
    lo = jnp.where(valid, jnp.maximum(offs[ex], tile * MOE_TILE) - tile * MOE_TILE, 0)
    hi = jnp.where(valid, jnp.minimum(ends[ex], (tile + 1) * MOE_TILE) - tile * MOE_TILE, 0)
    return pos, tile.astype(jnp.int32), ex.astype(jnp.int32), lo.astype(jnp.int32), hi.astype(jnp.int32)


def _pack_bf16_pair(y):
    n = y.shape[1] // 2
    hi = lax.bitcast_convert_type(y[:, :n].astype(BF16).astype(F32), jnp.uint32)
    lo = lax.bitcast_convert_type(y[:, n:].astype(BF16).astype(F32), jnp.uint32)
    return hi | (lo >> 16)


def _unpack_bf16_pair(u):
    hi = lax.bitcast_convert_type(u & jnp.uint32(0xFFFF0000), F32)
    lo = lax.bitcast_convert_type(u << 16, F32)
    return hi, lo


def _gmm_kernel(pos_ref, tile_ref, exp_ref, lo_ref, hi_ref, h_ref, wgu_ref, wd_ref, ys_ref,
                inv_ref, xbuf, acc_ref, sems):
    j = pl.program_id(0)
    n_items = pl.num_programs(0)
    tm = xbuf.shape[1]
    tokens = pos_ref.shape[0] // 2
    cur = tile_ref[j]
    nxt = tile_ref[jnp.minimum(j + 1, n_items - 1)]
    first = (j == 0) | (cur != tile_ref[jnp.maximum(j - 1, 0)])
    last = (j == n_items - 1) | (nxt != cur)

    def row_copies(tile):
        slot = tile % 2
        return [pltpu.make_async_copy(h_ref.at[pl.ds(inv_ref[tile * tm + r], 1), :],
                                      xbuf.at[slot, pl.ds(r, 1), :], sems.at[slot])
                for r in range(tm)]

    @pl.when(j == 0)
    def _():
        def fill(t, carry):
            inv_ref[pos_ref[t]] = t
            inv_ref[pos_ref[tokens + t]] = t
            return carry
        lax.fori_loop(0, tokens, fill, 0, unroll=8)
        for cp in row_copies(cur):
            cp.start()

    @pl.when((j < n_items - 1) & (nxt != cur))
    def _():
        for cp in row_copies(nxt):
            cp.start()

    @pl.when(first)
    def _():
        for cp in row_copies(cur):
            cp.wait()

    f = wd_ref.shape[0]
    x = xbuf[cur % 2].astype(BF16)
    hg = jnp.dot(x, wgu_ref[...].astype(BF16), preferred_element_type=F32)
    gate = hg[:, :f]
    row = lax.broadcasted_iota(jnp.int32, gate.shape, 0)
    mine = (row >= lo_ref[j]) & (row < hi_ref[j])
    act = jnp.where(mine, gate * jax.nn.sigmoid(gate) * hg[:, f:], 0.0)
    y = jnp.dot(act.astype(BF16), wd_ref[...].astype(BF16), preferred_element_type=F32)

    @pl.when(first)
    def _():
        acc_ref[...] = y

    @pl.when(jnp.logical_not(first))
    def _():
        acc_ref[...] += y

    @pl.when(last)
    def _():
        ys_ref[...] = _pack_bf16_pair(acc_ref[...])


def _gmm(plan, h, w_gu, w_down, layer):
    pos, tile, ex, lo, hi = plan
    n_rows = pos.shape[0]
    d = h.shape[1]
    f2 = w_gu.shape[-1]
    f = w_down.shape[2]
    return pl.pallas_call(
        _gmm_kernel,
        grid_spec=pltpu.PrefetchScalarGridSpec(
            num_scalar_prefetch=5,
            grid=(tile.shape[0],),
            in_specs=[pl.BlockSpec(memory_space=pl.ANY),
                      pl.BlockSpec((None, None, d, f2), lambda j, pos, tile, ex, lo, hi: (layer, ex[j], 0, 0)),
                      pl.BlockSpec((None, None, f, d), lambda j, pos, tile, ex, lo, hi: (layer, ex[j], 0, 0))],
            out_specs=pl.BlockSpec((MOE_TILE, d // 2), lambda j, pos, tile, ex, lo, hi: (tile[j], 0)),
            scratch_shapes=[pltpu.SMEM((n_rows,), jnp.int32),
                            pltpu.VMEM((2, MOE_TILE, d), F32),
                            pltpu.VMEM((MOE_TILE, d), F32),
                            pltpu.SemaphoreType.DMA((2,))]),
        out_shape=jax.ShapeDtypeStruct((n_rows, d // 2), jnp.uint32),
        compiler_params=_params(dimension_semantics=("arbitrary",)),
        name="moe_grouped_matmul",
    )(pos, tile, ex, lo, hi, h, w_gu, w_down)


def _ple_kernel(pos_ref, h_ref, rw_ref, ys_ref, g_ref, b_ref, wgate_ref, p_ref, wp_ref, x_ref, xb_ref,
                ybuf, sems, wgate_b_ref, wp_b_ref, *, alpha):
    i = pl.program_id(0)
    _cast_weights_once(i, [(wgate_ref, wgate_b_ref), (wp_ref, wp_b_ref)])
    n_tiles = pl.num_programs(0) - 1
    tm = h_ref.shape[0]
    tokens = pos_ref.shape[0] // 2

    def row_copies(tile):
        slot = tile % 2
        return [pltpu.make_async_copy(ys_ref.at[pl.ds(pos_ref[s * tokens + tile * tm + r], 1), :],
                                      ybuf.at[slot, s, pl.ds(r, 1), :], sems.at[slot])
                for r in range(tm) for s in range(2)]

    @pl.when(i < n_tiles)
    def _():
        for cp in row_copies(i):
            cp.start()

    @pl.when(i > 0)
    def _():
        for cp in row_copies(i - 1):
            cp.wait()
        slot = (i - 1) % 2
        rw = rw_ref[...]
        y0_hi, y0_lo = _unpack_bf16_pair(ybuf[slot, 0])
        y1_hi, y1_lo = _unpack_bf16_pair(ybuf[slot, 1])
        w0, w1 = rw[:, 0:1], rw[:, 1:2]
        f = jnp.concatenate([w0 * y0_hi + w1 * y1_hi, w0 * y0_lo + w1 * y1_lo], axis=1)
        h2 = _layer_norm(alpha * h_ref[...] + f, g_ref[...], b_ref[...])
        gate = jax.nn.sigmoid(jnp.dot(h2.astype(BF16), wgate_b_ref[...], preferred_element_type=F32))
        pe = jnp.dot(p_ref[...].astype(BF16), wp_b_ref[...], preferred_element_type=F32)
        x_new = h2 + gate * pe
        x_ref[...] = x_new
        xb_ref[...] = x_new.astype(BF16)


def _ple(pos, h, rw, ys, ln_g, ln_b, w_gate, p, w_ple, layer, alpha, tm=256):
    t, d = h.shape
    pd = p.shape[-1]
    prev = lambda i: jnp.maximum(i - 1, 0)
    vec = pl.BlockSpec((None, 1, d), lambda i, pos: (layer, 0, 0))
    row = lambda w: pl.BlockSpec((tm, w), lambda i, pos: (prev(i), 0))
    return pl.pallas_call(
        functools.partial(_ple_kernel, alpha=alpha),
        grid_spec=pltpu.PrefetchScalarGridSpec(
            num_scalar_prefetch=1,
            grid=(t // tm + 1,),
            in_specs=[row(d), row(ROUTER_LANES), pl.BlockSpec(memory_space=pl.ANY), vec, vec,
                      pl.BlockSpec((None, d, d), lambda i, pos: (layer, 0, 0), pipeline_mode=pl.Buffered(1)),
                      pl.BlockSpec((None, tm, pd), lambda i, pos: (layer, prev(i), 0)),
                      pl.BlockSpec((None, pd, d), lambda i, pos: (layer, 0, 0), pipeline_mode=pl.Buffered(1))],
            out_specs=[row(d), row(d)],
            scratch_shapes=[pltpu.VMEM((2, 2, tm, d // 2), jnp.uint32), pltpu.SemaphoreType.DMA((2,)),
                            pltpu.VMEM((d, d), BF16), pltpu.VMEM((pd, d), BF16)]),
        out_shape=[jax.ShapeDtypeStruct((t, d), F32), jax.ShapeDtypeStruct((t, d), BF16)],
        compiler_params=_params(dimension_semantics=("arbitrary",)),
        name="gather_ln2_ple",
    )(pos, h, rw, ys, ln_g, ln_b, w_gate, p, w_ple)


def kernel(x, p, w_in, conv_w, conv_b, conv_ln_g, conv_ln_b, w_out, rel_bias, ln1_g, ln1_b, router_g_w, router_g_b, router_e_w, router_e_b, expert_w_gu, expert_w_down, ln2_g, ln2_b, ple_w, ple_gate_w):
    batch, seq, d = x.shape
    depth = w_in.shape[0]
    n_heads = rel_bias.shape[1]
    attn_w = n_heads * HEAD_DIM
    conv_width = conv_w.shape[2]
    t = batch * seq
    alpha = (2 * depth) ** 0.25

    pad = ROUTER_LANES - N_GROUPS - N_EXPERTS
    w_router = jnp.pad(jnp.concatenate([router_g_w, router_e_w], axis=-1), ((0, 0), (0, 0), (0, pad)))
    w_router_hi = w_router.astype(BF16)
    w_router_lo = (w_router - w_router_hi.astype(F32)).astype(BF16)
    w_router = jnp.concatenate([w_router_hi, w_router_lo], axis=-1)
    b_router = jnp.pad(jnp.concatenate([router_g_b, router_e_b], axis=-1), ((0, 0), (0, pad)))[:, None, :]
    vec3 = lambda v: v[:, None, :]
    p2 = p.reshape(depth, t, p.shape[-1])

    bias = _bias_tiles(rel_bias)
    xf = x.reshape(t, d)
    xb = xf.astype(BF16)
    for i in range(depth):
        qk = _proj_qk(xb, w_in, i, tn=attn_w)
        vt = _proj_vt(xb, w_in, i, batch, attn_w, col_block=2)
        u = _proj_glu(xb, w_in, i, a_col=3 * attn_w, width=conv_width)
        attn = _attention(qk, vt, bias, batch, n_heads)
        conv = _conv_module(u, conv_w, vec3(conv_b), vec3(conv_ln_g), vec3(conv_ln_b), i, batch)
        h, route_i, route_w, cnt = _outproj(attn, conv, w_out, xf, vec3(ln1_g), vec3(ln1_b),
                                            w_router, b_router, i, alpha)
        plan = _moe_plan(route_i, cnt, 2 * t)
        ys = _gmm(plan, h, expert_w_gu, expert_w_down, i)
        xf, xb = _ple(plan[0], h, route_w, ys, vec3(ln2_g), vec3(ln2_b), ple_gate_w, p2, ple_w, i, alpha)
    return xf.reshape(batch, seq, d)
```

```python
import functools
import math

import numpy as np
import jax
import jax.numpy as jnp
from jax import lax
from jax.experimental import pallas as pl
from jax.experimental.pallas import tpu as pltpu

F32 = jnp.float32
BF16 = jnp.bfloat16

HEAD_DIM = 128
BLOCK = 256
TOP_BLOCKS = 3
CONV_K = 31
N_BUCKETS = 32
MAX_DISTANCE = 128
N_GROUPS = 4
EXPERTS_PER_GROUP = 8
N_EXPERTS = N_GROUPS * EXPERTS_PER_GROUP
LN_EPS = 1e-5
LOG2_E = math.log2(math.e)
NEG = -1e30
ROUTER_LANES = 128
VMEM_LIMIT = 56 * 1024 * 1024


def _bucket_thresholds():
    n = np.arange(0, 4 * MAX_DISTANCE)
    max_exact = N_BUCKETS // 2
    nf = np.maximum(n, max_exact).astype(np.float32)
    large = max_exact + (np.log(nf / np.float32(max_exact)) / np.float32(math.log(MAX_DISTANCE / max_exact))
                         * np.float32(N_BUCKETS - max_exact)).astype(np.int32)
    bucket = np.where(n < max_exact, n, np.minimum(large, N_BUCKETS - 1))
    assert np.all(np.diff(bucket) >= 0) and bucket[-1] == N_BUCKETS - 1
    return [int(np.argmax(bucket >= b)) for b in range(N_BUCKETS)]


_BUCKET_START = _bucket_thresholds()


def _params(**kw):
    return pltpu.CompilerParams(vmem_limit_bytes=VMEM_LIMIT, **kw)


def _bias_kernel(rb_ref, out_ref):
    h = pl.program_id(0)
    kj = lax.broadcasted_iota(jnp.int32, (BLOCK, BLOCK), 0)
    qi = lax.broadcasted_iota(jnp.int32, (BLOCK, BLOCK), 1)
    for d in range(3):
        rel = qi - kj + d * BLOCK
        val = jnp.full((BLOCK, BLOCK), rb_ref[0, h], F32)
        for b in range(1, N_BUCKETS):
            val = jnp.where(rel >= _BUCKET_START[b], rb_ref[b, h], val)
        val = val * LOG2_E
        if d == 0:
            val = jnp.where(rel >= 0, val, NEG)
        out_ref[d] = val


def _bias_tiles(rel_bias):
    n_heads = rel_bias.shape[1]
    assert _BUCKET_START[-1] <= BLOCK + 1
    return pl.pallas_call(
        _bias_kernel,
        grid=(n_heads,),
        in_specs=[pl.BlockSpec(memory_space=pltpu.SMEM)],
        out_specs=pl.BlockSpec((None, 3, BLOCK, BLOCK), lambda h: (h, 0, 0, 0)),
        out_shape=jax.ShapeDtypeStruct((n_heads, 3, BLOCK, BLOCK), F32),
        name="t5_bias_tiles",
    )(rel_bias)


def _cast_weights_once(row_tile_id, pairs):
    @pl.when(row_tile_id == 0)
    def _():
        for src, dst in pairs:
            dst[...] = src[...].astype(BF16)


def _mm_kernel(x_ref, w_ref, o_ref, wb_ref):
    _cast_weights_once(pl.program_id(1), [(w_ref, wb_ref)])
    o_ref[...] = jnp.dot(x_ref[...], wb_ref[...], preferred_element_type=F32).astype(o_ref.dtype)


def _proj_qk(xb, w_in, layer, tm=512, tn=1024):
    t, d = xb.shape
    n_out = 2 * tn
    return pl.pallas_call(
        _mm_kernel,
        grid=(n_out // tn, t // tm),
        in_specs=[pl.BlockSpec((tm, d), lambda j, i: (i, 0)),
                  pl.BlockSpec((None, d, tn), lambda j, i: (layer, 0, j))],
        out_specs=pl.BlockSpec((tm, tn), lambda j, i: (i, j)),
        out_shape=jax.ShapeDtypeStruct((t, n_out), BF16),
        scratch_shapes=[pltpu.VMEM((d, tn), BF16)],
        compiler_params=_params(dimension_semantics=("arbitrary", "arbitrary")),
        name="proj_qk",
    )(xb, w_in)


def _proj_vt_kernel(x_ref, w_ref, o_ref, wb_ref):
    _cast_weights_once(pl.program_id(0), [(w_ref, wb_ref)])
    r = jnp.dot(x_ref[...], wb_ref[...], preferred_element_type=F32)
    o_ref[...] = r.T.astype(o_ref.dtype)


def _proj_vt(xb, w_in, layer, batch, width, col_block, tm=512):
    t, d = xb.shape
    seq = t // batch
    per_b = seq // tm
    return pl.pallas_call(
        _proj_vt_kernel,
        grid=(t // tm,),
        in_specs=[pl.BlockSpec((tm, d), lambda i: (i, 0)),
                  pl.BlockSpec((None, d, width), lambda i: (layer, 0, col_block))],
        out_specs=pl.BlockSpec((None, width, tm), lambda i: (i // per_b, 0, i % per_b)),
        out_shape=jax.ShapeDtypeStruct((batch, width, seq), BF16),
        scratch_shapes=[pltpu.VMEM((d, width), BF16)],
        compiler_params=_params(dimension_semantics=("arbitrary",)),
        name="proj_vt",
    )(xb, w_in)


def _proj_glu_kernel(x_ref, wa_ref, wg_ref, o_ref, wab_ref, wgb_ref):
    _cast_weights_once(pl.program_id(1), [(wa_ref, wab_ref), (wg_ref, wgb_ref)])
    x = x_ref[...]
    a = jnp.dot(x, wab_ref[...], preferred_element_type=F32)
    g = jnp.dot(x, wgb_ref[...], preferred_element_type=F32)
    o_ref[...] = a * jax.nn.sigmoid(g)


def _proj_glu(xb, w_in, layer, a_col, width, tm=512, tn=512):
    t, d = xb.shape
    a_blk = a_col // tn
    g_blk = (a_col + width) // tn
    return pl.pallas_call(
        _proj_glu_kernel,
        grid=(width // tn, t // tm),
        in_specs=[pl.BlockSpec((tm, d), lambda j, i: (i, 0)),
                  pl.BlockSpec((None, d, tn), lambda j, i: (layer, 0, a_blk + j)),
                  pl.BlockSpec((None, d, tn), lambda j, i: (layer, 0, g_blk + j))],
        out_specs=pl.BlockSpec((tm, tn), lambda j, i: (i, j)),
        out_shape=jax.ShapeDtypeStruct((t, width), F32),
        scratch_shapes=[pltpu.VMEM((d, tn), BF16), pltpu.VMEM((d, tn), BF16)],
        compiler_params=_params(dimension_semantics=("arbitrary", "arbitrary")),
        name="proj_glu",
    )(xb, w_in, w_in)


ATTN_HEADS_PER_STEP = 2
_NT = (((1,), (1,)), ((), ()))


def _block_penalty(kmean, q, c):
    gate = lax.dot_general(kmean, q.astype(F32), _NT, precision=lax.Precision.HIGHEST,
                           preferred_element_type=F32)
    row = lax.broadcasted_iota(jnp.int32, gate.shape, 0)
    gate_m = jnp.where(row < c, gate, -jnp.inf)
    pens = []
    for n in range(c):
        g_n = gate[n:n + 1, :]
        beats = (gate_m > g_n) | ((gate_m == g_n) & (row < n))
        rank = jnp.sum(beats.astype(F32), axis=0, keepdims=True)
        pens.append(jnp.where(rank < TOP_BLOCKS, 0.0, NEG))
    return pens


def _attn_row(c, hh, q_ref, k_ref, vt_ref, bias_ref, o_ref, kmean_ref, scale):
    hs = slice(hh * HEAD_DIM, (hh + 1) * HEAD_DIM)
    q = q_ref[:, hs]
    keys = (c + 1) * BLOCK
    s = lax.dot_general(k_ref[0:keys, hs], q, _NT, preferred_element_type=F32)
    pens = _block_penalty(kmean_ref[hh], q, c) if c > TOP_BLOCKS else None
    blocks = []
    for n in range(c + 1):
        b = s[n * BLOCK:(n + 1) * BLOCK, :] * scale + bias_ref[hh, min(c - n, 2)]
        if pens is not None and n < c:
            b = b + pens[n]
        blocks.append(b)
    m = jnp.max(functools.reduce(jnp.maximum, blocks), axis=0, keepdims=True)
    ps = [jnp.exp2(b - m) for b in blocks]
    l = jnp.sum(functools.reduce(jnp.add, ps), axis=0, keepdims=True)
    p = jnp.concatenate([x.astype(BF16) for x in ps], axis=0)
    acc = jnp.dot(vt_ref[hs, 0:keys], p, preferred_element_type=F32)
    o_ref[:, hs] = (acc / l).T.astype(o_ref.dtype)


def _attn_kernel(q_ref, k_ref, vt_ref, bias_ref, o_ref, kmean_ref, *, n_blocks, scale):
    c_id = pl.program_id(2)
    heads = q_ref.shape[1] // HEAD_DIM

    @pl.when(c_id == 0)
    def _():
        for hh in range(heads):
            for n in range(n_blocks):
                kb = k_ref[n * BLOCK:(n + 1) * BLOCK, hh * HEAD_DIM:(hh + 1) * HEAD_DIM]
                kmean_ref[hh, n:n + 1, :] = jnp.mean(kb.astype(F32), axis=0, keepdims=True)

    for c in range(n_blocks):
        @pl.when(c_id == c)
        def _(c=c):
            for hh in range(heads):
                _attn_row(c, hh, q_ref, k_ref, vt_ref, bias_ref, o_ref, kmean_ref, scale)


def _attention(qk, vt, bias, batch, n_heads):
    t = qk.shape[0]
    seq = t // batch
    nb = seq // BLOCK
    hp = ATTN_HEADS_PER_STEP
    w = hp * HEAD_DIM
    kern = functools.partial(_attn_kernel, n_blocks=nb, scale=HEAD_DIM ** -0.5 * LOG2_E)
    return pl.pallas_call(
        kern,
        grid=(batch, n_heads // hp, nb),
        in_specs=[pl.BlockSpec((BLOCK, w), lambda b, h, c: (b * nb + c, h)),
                  pl.BlockSpec((seq, w), lambda b, h, c: (b, n_heads // hp + h)),
                  pl.BlockSpec((None, w, seq), lambda b, h, c: (b, h, 0)),
                  pl.BlockSpec((hp, 3, BLOCK, BLOCK), lambda b, h, c: (h, 0, 0, 0))],
        out_specs=pl.BlockSpec((BLOCK, w), lambda b, h, c: (b * nb + c, h)),
        out_shape=jax.ShapeDtypeStruct((t, n_heads * HEAD_DIM), BF16),
        scratch_shapes=[pltpu.VMEM((hp, nb, HEAD_DIM), F32)],
        compiler_params=_params(),
        name="moba_attention",
    )(qk, qk, vt, bias)


CONV_TILE = 256
CONV_HALO = 32
CONV_ROWS = 64
LANES = 128
SUBLANES = 8


def _conv_kernel(prev_ref, cur_ref, w_ref, b_ref, g_ref, beta_ref, o_ref, win_ref, y_ref):
    s = pl.program_id(1)
    win_ref[0:CONV_HALO, :] = jnp.where(s > 0, prev_ref[...], 0.0)
    win_ref[CONV_HALO:, :] = cur_ref[...]
    width = cur_ref.shape[1]
    first = CONV_HALO - (CONV_K - 1)
    phases = [[k for k in range(CONV_K) if (first + k) % SUBLANES == a] for a in range(SUBLANES)]
    for lc in range(width // LANES):
        ls = slice(lc * LANES, (lc + 1) * LANES)
        for rc in range(CONV_TILE // CONV_ROWS):
            r0 = rc * CONV_ROWS
            acc = jnp.broadcast_to(b_ref[:, ls], (CONV_ROWS, LANES))
            for a, taps in enumerate(phases):
                rows = CONV_ROWS + (SUBLANES if a else 0)
                z = None
                for k in taps:
                    start = r0 + first + k - a
                    term = win_ref[start:start + rows, ls] * w_ref[k:k + 1, ls]
                    z = term if z is None else z + term
                acc = acc + z[a:a + CONV_ROWS, :]
            y_ref[r0:r0 + CONV_ROWS, ls] = acc
    y = y_ref[...]
    mu = jnp.mean(y, axis=-1, keepdims=True)
    yc = y - mu
    var = jnp.mean(yc * yc, axis=-1, keepdims=True)
    z = yc * lax.rsqrt(var + LN_EPS) * g_ref[...] + beta_ref[...]
    o_ref[...] = (z * jax.nn.sigmoid(z)).astype(o_ref.dtype)


def _conv_module(u, conv_w, conv_b, ln_g, ln_b, layer, batch):
    t, width = u.shape
    seq = t // batch
    tiles = seq // CONV_TILE
    halo_per_tile = CONV_TILE // CONV_HALO
    vec = pl.BlockSpec((None, 1, width), lambda b, s: (layer, 0, 0))
    return pl.pallas_call(
        _conv_kernel,
        grid=(batch, tiles),
        in_specs=[pl.BlockSpec((CONV_HALO, width),
                               lambda b, s: (jnp.maximum((b * tiles + s) * halo_per_tile - 1, 0), 0)),
                  pl.BlockSpec((CONV_TILE, width), lambda b, s: (b * tiles + s, 0)),
                  pl.BlockSpec((None, CONV_K, width), lambda b, s: (layer, 0, 0)),
                  vec, vec, vec],
        out_specs=pl.BlockSpec((CONV_TILE, width), lambda b, s: (b * tiles + s, 0)),
        out_shape=jax.ShapeDtypeStruct((t, width), BF16),
        scratch_shapes=[pltpu.VMEM((CONV_HALO + CONV_TILE, width), F32), pltpu.VMEM((CONV_TILE, width), F32)],
        compiler_params=_params(),
        name="conformer_conv",
    )(u, u, conv_w, conv_b, ln_g, ln_b)


def _layer_norm(y, g, b):
    mu = jnp.mean(y, axis=-1, keepdims=True)
    yc = y - mu
    var = jnp.mean(yc * yc, axis=-1, keepdims=True)
    return yc * lax.rsqrt(var + LN_EPS) * g + b


def _route(logits):
    lane = lax.broadcasted_iota(jnp.int32, logits.shape, 1)
    is_group = lane < N_GROUPS
    gl = jnp.where(is_group, logits, -jnp.inf)
    gmax = jnp.max(gl, axis=-1, keepdims=True)
    g_top = jnp.min(jnp.where(gl == gmax, lane, ROUTER_LANES), axis=-1, keepdims=True)
    p_group = 1.0 / jnp.sum(jnp.where(is_group, jnp.exp(gl - gmax), 0.0), axis=-1, keepdims=True)
    e_idx = lane - N_GROUPS
    in_group = (e_idx >= 0) & (e_idx < N_EXPERTS) & ((e_idx // EXPERTS_PER_GROUP) == g_top)
    el = jnp.where(in_group, logits, -jnp.inf)
    m1 = jnp.max(el, axis=-1, keepdims=True)
    i1 = jnp.min(jnp.where(el == m1, lane, ROUTER_LANES), axis=-1, keepdims=True)
    el2 = jnp.where(lane == i1, -jnp.inf, el)
    m2 = jnp.max(el2, axis=-1, keepdims=True)
    i2 = jnp.min(jnp.where(el2 == m2, lane, ROUTER_LANES), axis=-1, keepdims=True)
    e2 = jnp.exp(m2 - m1)
    w1 = p_group / (1.0 + e2)
    w2 = p_group * e2 / (1.0 + e2)
    return i1, i2, w1, w2


def _outproj_kernel(attn_ref, conv_ref, w_ref, x_ref, g_ref, b_ref, wr_ref, br_ref,
                    h_ref, ri_ref, rw_ref, cnt_ref, carry_ref, wb_ref, *, alpha):
    i = pl.program_id(0)
    _cast_weights_once(i, [(w_ref, wb_ref)])

    @pl.when(i == 0)
    def _():
        carry_ref[...] = jnp.zeros_like(carry_ref)

    ka = attn_ref.shape[1]
    mix = jnp.dot(attn_ref[...], wb_ref[0:ka, :], preferred_element_type=F32)
    mix = mix + jnp.dot(conv_ref[...], wb_ref[ka:, :], preferred_element_type=F32)
    h = _layer_norm(alpha * x_ref[...] + mix, g_ref[...], b_ref[...])
    h_ref[...] = h
    h_hi = h.astype(BF16)
    h_lo = (h - h_hi.astype(F32)).astype(BF16)
    t_hi = jnp.dot(h_hi, wr_ref[...], preferred_element_type=F32)
    t_lo = jnp.dot(h_lo, wr_ref[:, 0:ROUTER_LANES], preferred_element_type=F32)
    logits = t_hi[:, 0:ROUTER_LANES] + t_hi[:, ROUTER_LANES:] + t_lo + br_ref[...]
    i1, i2, w1, w2 = _route(logits)

    tm = logits.shape[0]
    lane = lax.broadcasted_iota(jnp.int32, logits.shape, 1)
    onehot = jnp.where((lane == i1) | (lane == i2), 1.0, 0.0)
    rr = lax.broadcasted_iota(jnp.int32, (tm, tm), 0)
    cc = lax.broadcasted_iota(jnp.int32, (tm, tm), 1)
    earlier = jnp.where(cc < rr, 1.0, 0.0).astype(BF16)
    before = jnp.dot(earlier, onehot.astype(BF16), preferred_element_type=F32) + carry_ref[...]
    r1 = jnp.sum(jnp.where(lane == i1, before, 0.0), axis=-1, keepdims=True)
    r2 = jnp.sum(jnp.where(lane == i2, before, 0.0), axis=-1, keepdims=True)
    carry_ref[...] += jnp.sum(onehot, axis=0, keepdims=True)
    cnt_ref[...] = jnp.broadcast_to(carry_ref[...], cnt_ref.shape)
    rec = jnp.where(lane == 0, (i1 - N_GROUPS).astype(F32),
                    jnp.where(lane == 1, (i2 - N_GROUPS).astype(F32),
                              jnp.where(lane == 2, r1, jnp.where(lane == 3, r2, 0.0))))
    ri_ref[...] = rec.T[0:ri_ref.shape[0], :].astype(jnp.int32)
    rw_ref[...] = jnp.where(lane == 0, w1, jnp.where(lane == 1, w2, 0.0))


def _outproj(attn, conv, w_out, x, ln_g, ln_b, w_router, b_router, layer, alpha, tm=256):
    t, d = x.shape
    ka, kc = attn.shape[1], conv.shape[1]
    vec = pl.BlockSpec((None, 1, d), lambda i: (layer, 0, 0))
    row = lambda w: pl.BlockSpec((tm, w), lambda i: (i, 0))
    return pl.pallas_call(
        functools.partial(_outproj_kernel, alpha=alpha),
        grid=(t // tm,),
        in_specs=[row(ka), row(kc),
                  pl.BlockSpec((None, ka + kc, d), lambda i: (layer, 0, 0), pipeline_mode=pl.Buffered(1)),
                  row(d), vec, vec,
                  pl.BlockSpec((None, d, 2 * ROUTER_LANES), lambda i: (layer, 0, 0)),
                  pl.BlockSpec((None, 1, ROUTER_LANES), lambda i: (layer, 0, 0))],
        out_specs=[row(d), pl.BlockSpec((SUBLANES, tm), lambda i: (0, i)), row(ROUTER_LANES),
                   pl.BlockSpec((8, ROUTER_LANES), lambda i: (0, 0))],
        out_shape=[jax.ShapeDtypeStruct((t, d), F32),
                   jax.ShapeDtypeStruct((SUBLANES, t), jnp.int32),
                   jax.ShapeDtypeStruct((t, ROUTER_LANES), F32),
                   jax.ShapeDtypeStruct((8, ROUTER_LANES), F32)],
        scratch_shapes=[pltpu.VMEM((1, ROUTER_LANES), F32), pltpu.VMEM((ka + kc, d), BF16)],
        compiler_params=_params(dimension_semantics=("arbitrary",)),
        name="outproj_ln_router",
    )(attn, conv, w_out, x, ln_g, ln_b, w_router, b_router)


MOE_TILE = 256


def _moe_plan(route_i, cnt, n_pairs):
    e = route_i[0:2, :]
    rank = route_i[2:4, :]
    counts = cnt[0, N_GROUPS:N_GROUPS + N_EXPERTS].astype(jnp.int32)
    tiles_e = (counts + MOE_TILE - 1) // MOE_TILE
    tile_end = jnp.cumsum(tiles_e)
    row_start = (tile_end - tiles_e) * MOE_TILE
    start = functools.reduce(jnp.add, [jnp.where(e == k, row_start[k], 0) for k in range(N_EXPERTS)])
    pos = (start + rank).reshape(-1)
    max_tiles = n_pairs // MOE_TILE + N_EXPERTS
    n_used = tile_end[-1:]
    j = jnp.minimum(jnp.arange(max_tiles, dtype=jnp.int32), jnp.maximum(n_used - 1, 0))
    ex = jnp.minimum(jnp.sum((j[:, None] >= tile_end[None, :]).astype(jnp.int32), axis=1), N_EXPERTS - 1)
    after = jnp.minimum(tile_end[ex], jnp.maximum(n_used - 1, 0))
    next_ex = jnp.where(tile_end[ex] < n_used, ex[after], ex)
    parity = jnp.cumsum(jnp.concatenate([jnp.zeros((1,), jnp.int32), (ex[1:] != ex[:-1]).astype(jnp.int32)])) % 2
    pad_lo = jnp.concatenate([row_start + counts, n_used * MOE_TILE])
    pad_hi = jnp.concatenate([tile_end * MOE_TILE, jnp.minimum(n_used + 1, max_tiles) * MOE_TILE])
    return (pos, ex.astype(jnp.int32), next_ex.astype(jnp.int32), parity.astype(jnp.int32),
            n_used.astype(jnp.int32), pad_lo.astype(jnp.int32), pad_hi.astype(jnp.int32))


def _pack_bf16_pair(y):
    n = y.shape[1] // 2
    hi = lax.bitcast_convert_type(y[:, :n].astype(BF16).astype(F32), jnp.uint32)
    lo = lax.bitcast_convert_type(y[:, n:].astype(BF16).astype(F32), jnp.uint32)
    return hi | (lo >> 16)


def _unpack_bf16_pair(u):
    hi = lax.bitcast_convert_type(u & jnp.uint32(0xFFFF0000), F32)
    lo = lax.bitcast_convert_type(u << 16, F32)
    return hi, lo


def _gmm_kernel(pos_ref, ex_ref, next_ref, parity_ref, nused_ref, padlo_ref, padhi_ref, h_ref, wgu_ref, wd_ref, ys_ref,
                inv_ref, xbuf, wgu_f_ref, wd_f_ref, wgu_b_ref, wd_b_ref, sems, wsems, *, layer):
    j = pl.program_id(0)
    n_tiles = pl.num_programs(0)
    tm = xbuf.shape[1]
    tokens = pos_ref.shape[0] // 2
    n_used = nused_ref[0]

    def row_copies(tile):
        slot = tile % 2
        return [pltpu.make_async_copy(h_ref.at[pl.ds(inv_ref[tile * tm + r], 1), :],
                                      xbuf.at[slot, pl.ds(r, 1), :], sems.at[slot])
                for r in range(tm)]

    @pl.when(j == 0)
    def _():
        def clear(r, carry):
            inv_ref[r] = 0
            return carry
        for k in range(padlo_ref.shape[0]):
            lax.fori_loop(padlo_ref[k], padhi_ref[k], clear, 0)

        def fill(t, carry):
            inv_ref[pos_ref[t]] = t
            inv_ref[pos_ref[tokens + t]] = t
            return carry
        lax.fori_loop(0, tokens, fill, 0, unroll=8)
        for cp in row_copies(0):
            cp.start()

    def weight_copies(expert, slot):
        return [pltpu.make_async_copy(wgu_ref.at[layer, expert], wgu_f_ref.at[slot], wsems.at[slot]),
                pltpu.make_async_copy(wd_ref.at[layer, expert], wd_f_ref.at[slot], wsems.at[slot])]

    @pl.when(j < n_used)
    def _():
        @pl.when((j == 0) | (ex_ref[j] != ex_ref[jnp.maximum(j - 1, 0)]))
        def _():
            slot = parity_ref[j]

            @pl.when(j == 0)
            def _():
                for cp in weight_copies(ex_ref[0], slot):
                    cp.start()

            for cp in weight_copies(ex_ref[j], slot):
                cp.wait()
            wgu_b_ref[...] = wgu_f_ref[slot].astype(BF16)
            wd_b_ref[...] = wd_f_ref[slot].astype(BF16)

            @pl.when(next_ref[j] != ex_ref[j])
            def _():
                for cp in weight_copies(next_ref[j], 1 - slot):
                    cp.start()

        for cp in row_copies(jnp.minimum(j + 1, n_tiles - 1)):
            cp.start()
        for cp in row_copies(j):
            cp.wait()
        f = wd_b_ref.shape[0]
        hg = jnp.dot(xbuf[j % 2].astype(BF16), wgu_b_ref[...], preferred_element_type=F32)
        gate = hg[:, :f]
        act = gate * jax.nn.sigmoid(gate) * hg[:, f:]
        y = jnp.dot(act.astype(BF16), wd_b_ref[...], preferred_element_type=F32)
        ys_ref[...] = _pack_bf16_pair(y)

    @pl.when(j >= n_used)
    def _():
        @pl.when(j == n_used)
        def _():
            for cp in row_copies(j):
                cp.wait()
        ys_ref[...] = jnp.zeros_like(ys_ref)


def _gmm(plan, h, w_gu, w_down, layer):
    pos, ex, next_ex, parity, n_used, pad_lo, pad_hi = plan
    max_tiles = ex.shape[0]
    d = h.shape[1]
    f2 = w_gu.shape[-1]
    f = w_down.shape[2]
    hbm = pl.BlockSpec(memory_space=pl.ANY)
    return pl.pallas_call(
        functools.partial(_gmm_kernel, layer=layer),
        grid_spec=pltpu.PrefetchScalarGridSpec(
            num_scalar_prefetch=7,
            grid=(max_tiles,),
            in_specs=[hbm, hbm, hbm],
            out_specs=pl.BlockSpec((MOE_TILE, d // 2), lambda j, *_: (j, 0)),
            scratch_shapes=[pltpu.SMEM((max_tiles * MOE_TILE,), jnp.int32),
                            pltpu.VMEM((2, MOE_TILE, d), F32),
                            pltpu.VMEM((2, d, f2), F32),
                            pltpu.VMEM((2, f, d), F32),
                            pltpu.VMEM((d, f2), BF16),
                            pltpu.VMEM((f, d), BF16),
                            pltpu.SemaphoreType.DMA((2,)),
                            pltpu.SemaphoreType.DMA((2,))]),
        out_shape=jax.ShapeDtypeStruct((max_tiles * MOE_TILE, d // 2), jnp.uint32),
        compiler_params=_params(dimension_semantics=("arbitrary",)),
        name="moe_grouped_matmul",
    )(pos, ex, next_ex, parity, n_used, pad_lo, pad_hi, h, w_gu, w_down)


def _ple_kernel(pos_ref, h_ref, rw_ref, ys_ref, g_ref, b_ref, wgate_ref, p_ref, wp_ref, x_ref, xb_ref,
                ybuf, sems, wgate_b_ref, wp_b_ref, *, alpha):
    i = pl.program_id(0)
    n_tiles = pl.num_programs(0)
    tm = h_ref.shape[0]
    tokens = pos_ref.shape[0] // 2

    def row_copies(tile, slot):
        return [pltpu.make_async_copy(ys_ref.at[pl.ds(pos_ref[s * tokens + tile * tm + r], 1), :],
                                      ybuf.at[slot, s, pl.ds(r, 1), :], sems.at[slot])
                for r in range(tm) for s in range(2)]

    @pl.when(i == 0)
    def _():
        wgate_b_ref[...] = wgate_ref[...].astype(BF16)
        wp_b_ref[...] = wp_ref[...].astype(BF16)
        for cp in row_copies(0, 0):
            cp.start()

    slot = i % 2
    ahead = jnp.minimum(i + 1, n_tiles - 1)
    for cp in row_copies(ahead, 1 - slot):
        cp.start()
    for cp in row_copies(i, slot):
        cp.wait()
    rw = rw_ref[...]
    y0_hi, y0_lo = _unpack_bf16_pair(ybuf[slot, 0])
    y1_hi, y1_lo = _unpack_bf16_pair(ybuf[slot, 1])
    w0, w1 = rw[:, 0:1], rw[:, 1:2]
    f = jnp.concatenate([w0 * y0_hi + w1 * y1_hi, w0 * y0_lo + w1 * y1_lo], axis=1)
    h2 = _layer_norm(alpha * h_ref[...] + f, g_ref[...], b_ref[...])
    gate = jax.nn.sigmoid(jnp.dot(h2.astype(BF16), wgate_b_ref[...], preferred_element_type=F32))
    pe = jnp.dot(p_ref[...].astype(BF16), wp_b_ref[...], preferred_element_type=F32)
    x_new = h2 + gate * pe
    x_ref[...] = x_new
    xb_ref[...] = x_new.astype(BF16)

    @pl.when(i == n_tiles - 1)
    def _():
        for cp in row_copies(ahead, 1 - slot):
            cp.wait()


def _ple(pos, h, rw, ys, ln_g, ln_b, w_gate, p, w_ple, layer, alpha, tm=256):
    t, d = h.shape
    pd = p.shape[-1]
    vec = pl.BlockSpec((None, 1, d), lambda i, pos: (layer, 0, 0))
    row = lambda w: pl.BlockSpec((tm, w), lambda i, pos: (i, 0))
    return pl.pallas_call(
        functools.partial(_ple_kernel, alpha=alpha),
        grid_spec=pltpu.PrefetchScalarGridSpec(
            num_scalar_prefetch=1,
            grid=(t // tm,),
            in_specs=[row(d), row(ROUTER_LANES), pl.BlockSpec(memory_space=pl.ANY), vec, vec,
                      pl.BlockSpec((None, d, d), lambda i, pos: (layer, 0, 0), pipeline_mode=pl.Buffered(1)),
                      pl.BlockSpec((None, tm, pd), lambda i, pos: (layer, i, 0)),
                      pl.BlockSpec((None, pd, d), lambda i, pos: (layer, 0, 0), pipeline_mode=pl.Buffered(1))],
            out_specs=[row(d), row(d)],
            scratch_shapes=[pltpu.VMEM((2, 2, tm, d // 2), jnp.uint32), pltpu.SemaphoreType.DMA((2,)),
                            pltpu.VMEM((d, d), BF16), pltpu.VMEM((pd, d), BF16)]),
        out_shape=[jax.ShapeDtypeStruct((t, d), F32), jax.ShapeDtypeStruct((t, d), BF16)],
        compiler_params=_params(dimension_semantics=("arbitrary",)),
        name="gather_ln2_ple",
    )(pos, h, rw, ys, ln_g, ln_b, w_gate, p, w_ple)


def kernel(x, p, w_in, conv_w, conv_b, conv_ln_g, conv_ln_b, w_out, rel_bias, ln1_g, ln1_b, router_g_w, router_g_b, router_e_w, router_e_b, expert_w_gu, expert_w_down, ln2_g, ln2_b, ple_w, ple_gate_w):
    batch, seq, d = x.shape
    depth = w_in.shape[0]
    n_heads = rel_bias.shape[1]
    attn_w = n_heads * HEAD_DIM
    conv_width = conv_w.shape[2]
    t = batch * seq
    alpha = (2 * depth) ** 0.25

    pad = ROUTER_LANES - N_GROUPS - N_EXPERTS
    w_router = jnp.pad(jnp.concatenate([router_g_w, router_e_w], axis=-1), ((0, 0), (0, 0), (0, pad)))
    w_router_hi = w_router.astype(BF16)
    w_router_lo = (w_router - w_router_hi.astype(F32)).astype(BF16)
    w_router = jnp.concatenate([w_router_hi, w_router_lo], axis=-1)
    b_router = jnp.pad(jnp.concatenate([router_g_b, router_e_b], axis=-1), ((0, 0), (0, pad)))[:, None, :]
    vec3 = lambda v: v[:, None, :]
    p2 = p.reshape(depth, t, p.shape[-1])

    bias = _bias_tiles(rel_bias)
    xf = x.reshape(t, d)
    xb = xf.astype(BF16)
    for i in range(depth):
        qk = _proj_qk(xb, w_in, i, tn=attn_w)
        vt = _proj_vt(xb, w_in, i, batch, attn_w, col_block=2)
        u = _proj_glu(xb, w_in, i, a_col=3 * attn_w, width=conv_width)
        attn = _attention(qk, vt, bias, batch, n_heads)
        conv = _conv_module(u, conv_w, vec3(conv_b), vec3(conv_ln_g), vec3(conv_ln_b), i, batch)
        h, route_i, route_w, cnt = _outproj(attn, conv, w_out, xf, vec3(ln1_g), vec3(ln1_b),
                                            w_router, b_router, i, alpha)
        plan = _moe_plan(route_i, cnt, 2 * t)
        ys = _gmm(plan, h, expert_w_gu, expert_w_down, i)
        xf, xb = _ple(plan[0], h, route_w, ys, vec3(ln2_g), vec3(ln2_b), ple_gate_w, p2, ple_w, i, alpha)
    return xf.reshape(batch, seq, d)
```

```python
import functools
import math

import numpy as np
import jax
import jax.numpy as jnp
from jax import lax
from jax.experimental import pallas as pl
from jax.experimental.pallas import tpu as pltpu

F32 = jnp.float32
BF16 = jnp.bfloat16

HEAD_DIM = 128
BLOCK = 256
TOP_BLOCKS = 3
CONV_K = 31
N_BUCKETS = 32
MAX_DISTANCE = 128
N_GROUPS = 4
EXPERTS_PER_GROUP = 8
N_EXPERTS = N_GROUPS * EXPERTS_PER_GROUP
LN_EPS = 1e-5
LOG2_E = math.log2(math.e)
NEG = -1e30
ROUTER_LANES = 128
VMEM_LIMIT = 56 * 1024 * 1024


def _bucket_thresholds():
    n = np.arange(0, 4 * MAX_DISTANCE)
    max_exact = N_BUCKETS // 2
    nf = np.maximum(n, max_exact).astype(np.float32)
    large = max_exact + (np.log(nf / np.float32(max_exact)) / np.float32(math.log(MAX_DISTANCE / max_exact))
                         * np.float32(N_BUCKETS - max_exact)).astype(np.int32)
    bucket = np.where(n < max_exact, n, np.minimum(large, N_BUCKETS - 1))
    assert np.all(np.diff(bucket) >= 0) and bucket[-1] == N_BUCKETS - 1
    return [int(np.argmax(bucket >= b)) for b in range(N_BUCKETS)]


_BUCKET_START = _bucket_thresholds()


def _params(**kw):
    return pltpu.CompilerParams(vmem_limit_bytes=VMEM_LIMIT, **kw)


def _bias_kernel(rb_ref, out_ref):
    h = pl.program_id(0)
    kj = lax.broadcasted_iota(jnp.int32, (BLOCK, BLOCK), 0)
    qi = lax.broadcasted_iota(jnp.int32, (BLOCK, BLOCK), 1)
    for d in range(3):
        rel = qi - kj + d * BLOCK
        val = jnp.full((BLOCK, BLOCK), rb_ref[0, h], F32)
        for b in range(1, N_BUCKETS):
            val = jnp.where(rel >= _BUCKET_START[b], rb_ref[b, h], val)
        val = val * LOG2_E
        if d == 0:
            val = jnp.where(rel >= 0, val, NEG)
        out_ref[d] = val


def _bias_tiles(rel_bias):
    n_heads = rel_bias.shape[1]
    assert _BUCKET_START[-1] <= BLOCK + 1
    return pl.pallas_call(
        _bias_kernel,
        grid=(n_heads,),
        in_specs=[pl.BlockSpec(memory_space=pltpu.SMEM)],
        out_specs=pl.BlockSpec((None, 3, BLOCK, BLOCK), lambda h: (h, 0, 0, 0)),
        out_shape=jax.ShapeDtypeStruct((n_heads, 3, BLOCK, BLOCK), F32),
        name="t5_bias_tiles",
    )(rel_bias)


def _cast_weights_once(row_tile_id, pairs):
    @pl.when(row_tile_id == 0)
    def _():
        for src, dst in pairs:
            dst[...] = src[...].astype(BF16)


def _mm_kernel(x_ref, w_ref, o_ref, wb_ref):
    _cast_weights_once(pl.program_id(1), [(w_ref, wb_ref)])
    o_ref[...] = jnp.dot(x_ref[...], wb_ref[...], preferred_element_type=F32).astype(o_ref.dtype)


def _proj_qk(xb, w_in, layer, tm=512, tn=1024):
    t, d = xb.shape
    n_out = 2 * tn
    return pl.pallas_call(
        _mm_kernel,
        grid=(n_out // tn, t // tm),
        in_specs=[pl.BlockSpec((tm, d), lambda j, i: (i, 0)),
                  pl.BlockSpec((None, d, tn), lambda j, i: (layer, 0, j))],
        out_specs=pl.BlockSpec((tm, tn), lambda j, i: (i, j)),
        out_shape=jax.ShapeDtypeStruct((t, n_out), BF16),
        scratch_shapes=[pltpu.VMEM((d, tn), BF16)],
        compiler_params=_params(dimension_semantics=("arbitrary", "arbitrary")),
        name="proj_qk",
    )(xb, w_in)


def _proj_vt_kernel(x_ref, w_ref, o_ref, wb_ref):
    _cast_weights_once(pl.program_id(0), [(w_ref, wb_ref)])
    r = jnp.dot(x_ref[...], wb_ref[...], preferred_element_type=F32)
    o_ref[...] = r.T.astype(o_ref.dtype)


def _proj_vt(xb, w_in, layer, batch, width, col_block, tm=512):
    t, d = xb.shape
    seq = t // batch
    per_b = seq // tm
    return pl.pallas_call(
        _proj_vt_kernel,
        grid=(t // tm,),
        in_specs=[pl.BlockSpec((tm, d), lambda i: (i, 0)),
                  pl.BlockSpec((None, d, width), lambda i: (layer, 0, col_block))],
        out_specs=pl.BlockSpec((None, width, tm), lambda i: (i // per_b, 0, i % per_b)),
        out_shape=jax.ShapeDtypeStruct((batch, width, seq), BF16),
        scratch_shapes=[pltpu.VMEM((d, width), BF16)],
        compiler_params=_params(dimension_semantics=("arbitrary",)),
        name="proj_vt",
    )(xb, w_in)


def _proj_glu_kernel(x_ref, wa_ref, wg_ref, o_ref, wab_ref, wgb_ref):
    _cast_weights_once(pl.program_id(1), [(wa_ref, wab_ref), (wg_ref, wgb_ref)])
    x = x_ref[...]
    a = jnp.dot(x, wab_ref[...], preferred_element_type=F32)
    g = jnp.dot(x, wgb_ref[...], preferred_element_type=F32)
    o_ref[...] = a * jax.nn.sigmoid(g)


def _proj_glu(xb, w_in, layer, a_col, width, tm=512, tn=512):
    t, d = xb.shape
    a_blk = a_col // tn
    g_blk = (a_col + width) // tn
    return pl.pallas_call(
        _proj_glu_kernel,
        grid=(width // tn, t // tm),
        in_specs=[pl.BlockSpec((tm, d), lambda j, i: (i, 0)),
                  pl.BlockSpec((None, d, tn), lambda j, i: (layer, 0, a_blk + j)),
                  pl.BlockSpec((None, d, tn), lambda j, i: (layer, 0, g_blk + j))],
        out_specs=pl.BlockSpec((tm, tn), lambda j, i: (i, j)),
        out_shape=jax.ShapeDtypeStruct((t, width), F32),
        scratch_shapes=[pltpu.VMEM((d, tn), BF16), pltpu.VMEM((d, tn), BF16)],
        compiler_params=_params(dimension_semantics=("arbitrary", "arbitrary")),
        name="proj_glu",
    )(xb, w_in, w_in)


ATTN_HEADS_PER_STEP = 2
_NT = (((1,), (1,)), ((), ()))


def _block_penalty(kmean, q, c):
    gate = lax.dot_general(kmean, q.astype(F32), _NT, precision=lax.Precision.HIGHEST,
                           preferred_element_type=F32)
    row = lax.broadcasted_iota(jnp.int32, gate.shape, 0)
    gate_m = jnp.where(row < c, gate, -jnp.inf)
    pens = []
    for n in range(c):
        g_n = gate[n:n + 1, :]
        beats = (gate_m > g_n) | ((gate_m == g_n) & (row < n))
        rank = jnp.sum(beats.astype(F32), axis=0, keepdims=True)
        pens.append(jnp.where(rank < TOP_BLOCKS, 0.0, NEG))
    return pens


def _attn_row(c, hh, q_ref, k_ref, vt_ref, bias_ref, o_ref, kmean_ref, scale):
    hs = slice(hh * HEAD_DIM, (hh + 1) * HEAD_DIM)
    q = q_ref[:, hs]
    keys = (c + 1) * BLOCK
    s = lax.dot_general(k_ref[0:keys, hs], q, _NT, preferred_element_type=F32)
    pens = _block_penalty(kmean_ref[hh], q, c) if c > TOP_BLOCKS else None
    blocks = []
    for n in range(c + 1):
        b = s[n * BLOCK:(n + 1) * BLOCK, :] * scale + bias_ref[hh, min(c - n, 2)]
        if pens is not None and n < c:
            b = b + pens[n]
        blocks.append(b)
    m = jnp.max(functools.reduce(jnp.maximum, blocks), axis=0, keepdims=True)
    ps = [jnp.exp2(b - m) for b in blocks]
    l = jnp.sum(functools.reduce(jnp.add, ps), axis=0, keepdims=True)
    p = jnp.concatenate([x.astype(BF16) for x in ps], axis=0)
    acc = jnp.dot(vt_ref[hs, 0:keys], p, preferred_element_type=F32)
    o_ref[:, hs] = (acc / l).T.astype(o_ref.dtype)


def _attn_kernel(q_ref, k_ref, vt_ref, bias_ref, o_ref, kmean_ref, *, n_blocks, scale):
    c_id = pl.program_id(2)
    heads = q_ref.shape[1] // HEAD_DIM

    @pl.when(c_id == 0)
    def _():
        for hh in range(heads):
            for n in range(n_blocks):
                kb = k_ref[n * BLOCK:(n + 1) * BLOCK, hh * HEAD_DIM:(hh + 1) * HEAD_DIM]
                kmean_ref[hh, n:n + 1, :] = jnp.mean(kb.astype(F32), axis=0, keepdims=True)

    for c in range(n_blocks):
        @pl.when(c_id == c)
        def _(c=c):
            for hh in range(heads):
                _attn_row(c, hh, q_ref, k_ref, vt_ref, bias_ref, o_ref, kmean_ref, scale)


def _attention(qk, vt, bias, batch, n_heads):
    t = qk.shape[0]
    seq = t // batch
    nb = seq // BLOCK
    hp = ATTN_HEADS_PER_STEP
    w = hp * HEAD_DIM
    kern = functools.partial(_attn_kernel, n_blocks=nb, scale=HEAD_DIM ** -0.5 * LOG2_E)
    return pl.pallas_call(
        kern,
        grid=(batch, n_heads // hp, nb),
        in_specs=[pl.BlockSpec((BLOCK, w), lambda b, h, c: (b * nb + c, h)),
                  pl.BlockSpec((seq, w), lambda b, h, c: (b, n_heads // hp + h)),
                  pl.BlockSpec((None, w, seq), lambda b, h, c: (b, h, 0)),
                  pl.BlockSpec((hp, 3, BLOCK, BLOCK), lambda b, h, c: (h, 0, 0, 0))],
        out_specs=pl.BlockSpec((BLOCK, w), lambda b, h, c: (b * nb + c, h)),
        out_shape=jax.ShapeDtypeStruct((t, n_heads * HEAD_DIM), BF16),
        scratch_shapes=[pltpu.VMEM((hp, nb, HEAD_DIM), F32)],
        compiler_params=_params(),
        name="moba_attention",
    )(qk, qk, vt, bias)


CONV_TILE = 256
CONV_HALO = 32
CONV_ROWS = 64
LANES = 128
SUBLANES = 8


def _conv_kernel(prev_ref, cur_ref, w_ref, b_ref, g_ref, beta_ref, o_ref, win_ref, y_ref):
    s = pl.program_id(1)
    win_ref[0:CONV_HALO, :] = jnp.where(s > 0, prev_ref[...], 0.0)
    win_ref[CONV_HALO:, :] = cur_ref[...]
    width = cur_ref.shape[1]
    first = CONV_HALO - (CONV_K - 1)
    phases = [[k for k in range(CONV_K) if (first + k) % SUBLANES == a] for a in range(SUBLANES)]
    for lc in range(width // LANES):
        ls = slice(lc * LANES, (lc + 1) * LANES)
        for rc in range(CONV_TILE // CONV_ROWS):
            r0 = rc * CONV_ROWS
            acc = jnp.broadcast_to(b_ref[:, ls], (CONV_ROWS, LANES))
            for a, taps in enumerate(phases):
                rows = CONV_ROWS + (SUBLANES if a else 0)
                z = None
                for k in taps:
                    start = r0 + first + k - a
                    term = win_ref[start:start + rows, ls] * w_ref[k:k + 1, ls]
                    z = term if z is None else z + term
                acc = acc + z[a:a + CONV_ROWS, :]
            y_ref[r0:r0 + CONV_ROWS, ls] = acc
    y = y_ref[...]
    mu = jnp.mean(y, axis=-1, keepdims=True)
    yc = y - mu
    var = jnp.mean(yc * yc, axis=-1, keepdims=True)
    z = yc * lax.rsqrt(var + LN_EPS) * g_ref[...] + beta_ref[...]
    o_ref[...] = (z * jax.nn.sigmoid(z)).astype(o_ref.dtype)


def _conv_module(u, conv_w, conv_b, ln_g, ln_b, layer, batch):
    t, width = u.shape
    seq = t // batch
    tiles = seq // CONV_TILE
    halo_per_tile = CONV_TILE // CONV_HALO
    vec = pl.BlockSpec((None, 1, width), lambda b, s: (layer, 0, 0))
    return pl.pallas_call(
        _conv_kernel,
        grid=(batch, tiles),
        in_specs=[pl.BlockSpec((CONV_HALO, width),
                               lambda b, s: (jnp.maximum((b * tiles + s) * halo_per_tile - 1, 0), 0)),
                  pl.BlockSpec((CONV_TILE, width), lambda b, s: (b * tiles + s, 0)),
                  pl.BlockSpec((None, CONV_K, width), lambda b, s: (layer, 0, 0)),
                  vec, vec, vec],
        out_specs=pl.BlockSpec((CONV_TILE, width), lambda b, s: (b * tiles + s, 0)),
        out_shape=jax.ShapeDtypeStruct((t, width), BF16),
        scratch_shapes=[pltpu.VMEM((CONV_HALO + CONV_TILE, width), F32), pltpu.VMEM((CONV_TILE, width), F32)],
        compiler_params=_params(),
        name="conformer_conv",
    )(u, u, conv_w, conv_b, ln_g, ln_b)


def _layer_norm(y, g, b):
    mu = jnp.mean(y, axis=-1, keepdims=True)
    yc = y - mu
    var = jnp.mean(yc * yc, axis=-1, keepdims=True)
    return yc * lax.rsqrt(var + LN_EPS) * g + b


def _route(logits):
    lane = lax.broadcasted_iota(jnp.int32, logits.shape, 1)
    is_group = lane < N_GROUPS
    gl = jnp.where(is_group, logits, -jnp.inf)
    gmax = jnp.max(gl, axis=-1, keepdims=True)
    g_top = jnp.min(jnp.where(gl == gmax, lane, ROUTER_LANES), axis=-1, keepdims=True)
    p_group = 1.0 / jnp.sum(jnp.where(is_group, jnp.exp(gl - gmax), 0.0), axis=-1, keepdims=True)
    e_idx = lane - N_GROUPS
    in_group = (e_idx >= 0) & (e_idx < N_EXPERTS) & ((e_idx // EXPERTS_PER_GROUP) == g_top)
    el = jnp.where(in_group, logits, -jnp.inf)
    m1 = jnp.max(el, axis=-1, keepdims=True)
    i1 = jnp.min(jnp.where(el == m1, lane, ROUTER_LANES), axis=-1, keepdims=True)
    el2 = jnp.where(lane == i1, -jnp.inf, el)
    m2 = jnp.max(el2, axis=-1, keepdims=True)
    i2 = jnp.min(jnp.where(el2 == m2, lane, ROUTER_LANES), axis=-1, keepdims=True)
    e2 = jnp.exp(m2 - m1)
    w1 = p_group / (1.0 + e2)
    w2 = p_group * e2 / (1.0 + e2)
    return i1, i2, w1, w2


def _outproj_kernel(attn_ref, conv_ref, w_ref, x_ref, g_ref, b_ref, wr_ref, br_ref,
                    h_ref, ri_ref, rw_ref, cnt_ref, carry_ref, wb_ref, *, alpha):
    i = pl.program_id(0)
    _cast_weights_once(i, [(w_ref, wb_ref)])

    @pl.when(i == 0)
    def _():
        carry_ref[...] = jnp.zeros_like(carry_ref)

    ka = attn_ref.shape[1]
    mix = jnp.dot(attn_ref[...], wb_ref[0:ka, :], preferred_element_type=F32)
    mix = mix + jnp.dot(conv_ref[...], wb_ref[ka:, :], preferred_element_type=F32)
    h = _layer_norm(alpha * x_ref[...] + mix, g_ref[...], b_ref[...])
    _store_row_chunks(h_ref, h)
    h_hi = h.astype(BF16)
    h_lo = (h - h_hi.astype(F32)).astype(BF16)
    t_hi = jnp.dot(h_hi, wr_ref[...], preferred_element_type=F32)
    t_lo = jnp.dot(h_lo, wr_ref[:, 0:ROUTER_LANES], preferred_element_type=F32)
    logits = t_hi[:, 0:ROUTER_LANES] + t_hi[:, ROUTER_LANES:] + t_lo + br_ref[...]
    i1, i2, w1, w2 = _route(logits)

    tm = logits.shape[0]
    lane = lax.broadcasted_iota(jnp.int32, logits.shape, 1)
    onehot = jnp.where((lane == i1) | (lane == i2), 1.0, 0.0)
    rr = lax.broadcasted_iota(jnp.int32, (tm, tm), 0)
    cc = lax.broadcasted_iota(jnp.int32, (tm, tm), 1)
    earlier = jnp.where(cc < rr, 1.0, 0.0).astype(BF16)
    before = jnp.dot(earlier, onehot.astype(BF16), preferred_element_type=F32) + carry_ref[...]
    r1 = jnp.sum(jnp.where(lane == i1, before, 0.0), axis=-1, keepdims=True)
    r2 = jnp.sum(jnp.where(lane == i2, before, 0.0), axis=-1, keepdims=True)
    carry_ref[...] += jnp.sum(onehot, axis=0, keepdims=True)
    cnt_ref[...] = jnp.broadcast_to(carry_ref[...], cnt_ref.shape)
    rec = jnp.where(lane == 0, (i1 - N_GROUPS).astype(F32),
                    jnp.where(lane == 1, (i2 - N_GROUPS).astype(F32),
                              jnp.where(lane == 2, r1, jnp.where(lane == 3, r2, 0.0))))
    ri_ref[...] = rec.T[0:ri_ref.shape[0], :].astype(jnp.int32)
    rw_ref[...] = jnp.where(lane == 0, w1, jnp.where(lane == 1, w2, 0.0))


def _outproj(attn, conv, w_out, x, ln_g, ln_b, w_router, b_router, layer, alpha, tm=256):
    t, d = x.shape
    ka, kc = attn.shape[1], conv.shape[1]
    vec = pl.BlockSpec((None, 1, d), lambda i: (layer, 0, 0))
    row = lambda w: pl.BlockSpec((tm, w), lambda i: (i, 0))
    return pl.pallas_call(
        functools.partial(_outproj_kernel, alpha=alpha),
        grid=(t // tm,),
        in_specs=[row(ka), row(kc),
                  pl.BlockSpec((None, ka + kc, d), lambda i: (layer, 0, 0), pipeline_mode=pl.Buffered(1)),
                  row(d), vec, vec,
                  pl.BlockSpec((None, d, 2 * ROUTER_LANES), lambda i: (layer, 0, 0)),
                  pl.BlockSpec((None, 1, ROUTER_LANES), lambda i: (layer, 0, 0))],
        out_specs=[pl.BlockSpec((tm * d // LANES, LANES), lambda i: (i, 0)),
                   pl.BlockSpec((SUBLANES, tm), lambda i: (0, i)), row(ROUTER_LANES),
                   pl.BlockSpec((8, ROUTER_LANES), lambda i: (0, 0))],
        out_shape=[jax.ShapeDtypeStruct((t * d // LANES, LANES), F32),
                   jax.ShapeDtypeStruct((SUBLANES, t), jnp.int32),
                   jax.ShapeDtypeStruct((t, ROUTER_LANES), F32),
                   jax.ShapeDtypeStruct((8, ROUTER_LANES), F32)],
        scratch_shapes=[pltpu.VMEM((1, ROUTER_LANES), F32), pltpu.VMEM((ka + kc, d), BF16)],
        compiler_params=_params(dimension_semantics=("arbitrary",)),
        name="outproj_ln_router",
    )(attn, conv, w_out, x, ln_g, ln_b, w_router, b_router)


MOE_TILE = 256


def _moe_plan(route_i, cnt, n_rows):
    e = route_i[0:2, :]
    rank = route_i[2:4, :]
    counts = cnt[0, N_GROUPS:N_GROUPS + N_EXPERTS].astype(jnp.int32)
    ends = jnp.cumsum(counts)
    offs = ends - counts
    start = functools.reduce(jnp.add, [jnp.where(e == k, offs[k], 0) for k in range(N_EXPERTS)])
    pos = (start + rank).reshape(-1)
    n_tiles = n_rows // MOE_TILE
    first_tile = offs // MOE_TILE
    n_items_e = jnp.where(counts > 0, (ends - 1) // MOE_TILE - first_tile + 1, 0)
    item_end = jnp.cumsum(n_items_e)
    item_start = item_end - n_items_e
    n_items = n_tiles + N_EXPERTS - 1
    j = jnp.arange(n_items, dtype=jnp.int32)
    valid = j < item_end[-1]
    jj = jnp.clip(j, 0, jnp.maximum(item_end[-1] - 1, 0))
    ex = jnp.minimum(jnp.sum((jj[:, None] >= item_end[None, :]).astype(jnp.int32), axis=1), N_EXPERTS - 1)
    tile = first_tile[ex] + (jj - item_start[ex])
    lo = jnp.where(valid, jnp.maximum(offs[ex], tile * MOE_TILE) - tile * MOE_TILE, 0)
    hi = jnp.where(valid, jnp.minimum(ends[ex], (tile + 1) * MOE_TILE) - tile * MOE_TILE, 0)
    return pos, tile.astype(jnp.int32), ex.astype(jnp.int32), lo.astype(jnp.int32), hi.astype(jnp.int32)


def _pack_bf16_pair(y):
    n = y.shape[1] // 2
    hi = lax.bitcast_convert_type(y[:, :n].astype(BF16).astype(F32), jnp.uint32)
    lo = lax.bitcast_convert_type(y[:, n:].astype(BF16).astype(F32), jnp.uint32)
    return hi | (lo >> 16)


def _unpack_bf16_pair(u):
    hi = lax.bitcast_convert_type(u & jnp.uint32(0xFFFF0000), F32)
    lo = lax.bitcast_convert_type(u << 16, F32)
    return hi, lo


def _store_row_chunks(ref, val):
    rows, width = val.shape
    n = width // LANES
    for c in range(n):
        ref[pl.ds(c, rows, stride=n), :] = val[:, c * LANES:(c + 1) * LANES]


def _load_row_chunks(ref, rows, n):
    return jnp.concatenate([ref[pl.ds(c, rows, stride=n), :] for c in range(n)], axis=1)


def _gmm_kernel(pos_ref, tile_ref, exp_ref, lo_ref, hi_ref, h_ref, wgu_ref, wd_ref, ys_ref,
                inv_ref, xbuf, acc_ref, wgu_b_ref, wd_b_ref, sems):
    j = pl.program_id(0)
    n_items = pl.num_programs(0)
    tm = acc_ref.shape[0]
    chunks = acc_ref.shape[1] // LANES
    tokens = pos_ref.shape[0] // 2
    cur = tile_ref[j]
    nxt = tile_ref[jnp.minimum(j + 1, n_items - 1)]
    first = (j == 0) | (cur != tile_ref[jnp.maximum(j - 1, 0)])
    last = (j == n_items - 1) | (nxt != cur)

    def row_copies(tile):
        slot = tile % 2
        return [pltpu.make_async_copy(h_ref.at[pl.ds(inv_ref[tile * tm + r] * chunks, chunks), :],
                                      xbuf.at[slot, pl.ds(r * chunks, chunks), :], sems.at[slot])
                for r in range(tm)]

    @pl.when(j == 0)
    def _():
        def fill(t, carry):
            inv_ref[pos_ref[t]] = t
            inv_ref[pos_ref[tokens + t]] = t
            return carry
        lax.fori_loop(0, tokens, fill, 0, unroll=8)
        for cp in row_copies(cur):
            cp.start()

    @pl.when((j < n_items - 1) & (nxt != cur))
    def _():
        for cp in row_copies(nxt):
            cp.start()

    @pl.when(first)
    def _():
        for cp in row_copies(cur):
            cp.wait()

    @pl.when((j == 0) | (exp_ref[j] != exp_ref[jnp.maximum(j - 1, 0)]))
    def _():
        wgu_b_ref[...] = wgu_ref[...].astype(BF16)
        wd_b_ref[...] = wd_ref[...].astype(BF16)

    f = wd_ref.shape[0]
    x = _load_row_chunks(xbuf.at[cur % 2], tm, chunks).astype(BF16)
    hg = jnp.dot(x, wgu_b_ref[...], preferred_element_type=F32)
    gate = hg[:, :f]
    row = lax.broadcasted_iota(jnp.int32, gate.shape, 0)
    mine = (row >= lo_ref[j]) & (row < hi_ref[j])
    act = jnp.where(mine, gate * jax.nn.sigmoid(gate) * hg[:, f:], 0.0)
    y = jnp.dot(act.astype(BF16), wd_b_ref[...], preferred_element_type=F32)

    @pl.when(first)
    def _():
        acc_ref[...] = y

    @pl.when(jnp.logical_not(first))
    def _():
        acc_ref[...] += y

    @pl.when(last)
    def _():
        _store_row_chunks(ys_ref, _pack_bf16_pair(acc_ref[...]))


def _gmm(plan, h_chunks, w_gu, w_down, layer):
    pos, tile, ex, lo, hi = plan
    n_rows = pos.shape[0]
    d = w_gu.shape[2]
    f2 = w_gu.shape[-1]
    f = w_down.shape[2]
    chunks = d // LANES
    return pl.pallas_call(
        _gmm_kernel,
        grid_spec=pltpu.PrefetchScalarGridSpec(
            num_scalar_prefetch=5,
            grid=(tile.shape[0],),
            in_specs=[pl.BlockSpec(memory_space=pl.ANY),
                      pl.BlockSpec((None, None, d, f2), lambda j, pos, tile, ex, lo, hi: (layer, ex[j], 0, 0)),
                      pl.BlockSpec((None, None, f, d), lambda j, pos, tile, ex, lo, hi: (layer, ex[j], 0, 0))],
            out_specs=pl.BlockSpec((MOE_TILE * chunks // 2, LANES),
                                   lambda j, pos, tile, ex, lo, hi: (tile[j], 0)),
            scratch_shapes=[pltpu.SMEM((n_rows,), jnp.int32),
                            pltpu.VMEM((2, MOE_TILE * chunks, LANES), F32),
                            pltpu.VMEM((MOE_TILE, d), F32),
                            pltpu.VMEM((d, f2), BF16),
                            pltpu.VMEM((f, d), BF16),
                            pltpu.SemaphoreType.DMA((2,))]),
        out_shape=jax.ShapeDtypeStruct((n_rows * chunks // 2, LANES), jnp.uint32),
        compiler_params=_params(dimension_semantics=("arbitrary",)),
        name="moe_grouped_matmul",
    )(pos, tile, ex, lo, hi, h_chunks, w_gu, w_down)


def _ple_kernel(pos_ref, h_ref, rw_ref, ys_ref, g_ref, b_ref, wgate_ref, p_ref, wp_ref, x_ref, xb_ref,
                ybuf, sems, wgate_b_ref, wp_b_ref, *, alpha):
    i = pl.program_id(0)
    n_tiles = pl.num_programs(0)
    tm = rw_ref.shape[0]
    tokens = pos_ref.shape[0] // 2
    yc = ybuf.shape[2] // tm
    hc = h_ref.shape[0] // tm

    def row_copies(tile, slot):
        return [pltpu.make_async_copy(ys_ref.at[pl.ds(pos_ref[s * tokens + tile * tm + r] * yc, yc), :],
                                      ybuf.at[slot, s, pl.ds(r * yc, yc), :], sems.at[slot])
                for r in range(tm) for s in range(2)]

    @pl.when(i == 0)
    def _():
        wgate_b_ref[...] = wgate_ref[...].astype(BF16)
        wp_b_ref[...] = wp_ref[...].astype(BF16)
        for cp in row_copies(0, 0):
            cp.start()

    slot = i % 2
    ahead = jnp.minimum(i + 1, n_tiles - 1)
    for cp in row_copies(ahead, 1 - slot):
        cp.start()
    for cp in row_copies(i, slot):
        cp.wait()
    rw = rw_ref[...]
    y0_hi, y0_lo = _unpack_bf16_pair(_load_row_chunks(ybuf.at[slot, 0], tm, yc))
    y1_hi, y1_lo = _unpack_bf16_pair(_load_row_chunks(ybuf.at[slot, 1], tm, yc))
    w0, w1 = rw[:, 0:1], rw[:, 1:2]
    f = jnp.concatenate([w0 * y0_hi + w1 * y1_hi, w0 * y0_lo + w1 * y1_lo], axis=1)
    h2 = _layer_norm(alpha * _load_row_chunks(h_ref, tm, hc) + f, g_ref[...], b_ref[...])
    gate = jax.nn.sigmoid(jnp.dot(h2.astype(BF16), wgate_b_ref[...], preferred_element_type=F32))
    pe = jnp.dot(p_ref[...].astype(BF16), wp_b_ref[...], preferred_element_type=F32)
    x_new = h2 + gate * pe
    x_ref[...] = x_new
    xb_ref[...] = x_new.astype(BF16)

    @pl.when(i == n_tiles - 1)
    def _():
        for cp in row_copies(ahead, 1 - slot):
            cp.wait()


def _ple(pos, h_chunks, rw, ys, ln_g, ln_b, w_gate, p, w_ple, layer, alpha, tm=256):
    d = w_gate.shape[-1]
    t = h_chunks.shape[0] * LANES // d
    pd = p.shape[-1]
    vec = pl.BlockSpec((None, 1, d), lambda i, pos: (layer, 0, 0))
    row = lambda w: pl.BlockSpec((tm, w), lambda i, pos: (i, 0))
    return pl.pallas_call(
        functools.partial(_ple_kernel, alpha=alpha),
        grid_spec=pltpu.PrefetchScalarGridSpec(
            num_scalar_prefetch=1,
            grid=(t // tm,),
            in_specs=[pl.BlockSpec((tm * d // LANES, LANES), lambda i, pos: (i, 0)),
                      row(ROUTER_LANES), pl.BlockSpec(memory_space=pl.ANY), vec, vec,
                      pl.BlockSpec((None, d, d), lambda i, pos: (layer, 0, 0), pipeline_mode=pl.Buffered(1)),
                      pl.BlockSpec((None, tm, pd), lambda i, pos: (layer, i, 0)),
                      pl.BlockSpec((None, pd, d), lambda i, pos: (layer, 0, 0), pipeline_mode=pl.Buffered(1))],
            out_specs=[row(d), row(d)],
            scratch_shapes=[pltpu.VMEM((2, 2, tm * d // (2 * LANES), LANES), jnp.uint32),
                            pltpu.SemaphoreType.DMA((2,)),
                            pltpu.VMEM((d, d), BF16), pltpu.VMEM((pd, d), BF16)]),
        out_shape=[jax.ShapeDtypeStruct((t, d), F32), jax.ShapeDtypeStruct((t, d), BF16)],
        compiler_params=_params(dimension_semantics=("arbitrary",)),
        name="gather_ln2_ple",
    )(pos, h_chunks, rw, ys, ln_g, ln_b, w_gate, p, w_ple)


def kernel(x, p, w_in, conv_w, conv_b, conv_ln_g, conv_ln_b, w_out, rel_bias, ln1_g, ln1_b, router_g_w, router_g_b, router_e_w, router_e_b, expert_w_gu, expert_w_down, ln2_g, ln2_b, ple_w, ple_gate_w):
    batch, seq, d = x.shape
    depth = w_in.shape[0]
    n_heads = rel_bias.shape[1]
    attn_w = n_heads * HEAD_DIM
    conv_width = conv_w.shape[2]
    t = batch * seq
    alpha = (2 * depth) ** 0.25

    pad = ROUTER_LANES - N_GROUPS - N_EXPERTS
    w_router = jnp.pad(jnp.concatenate([router_g_w, router_e_w], axis=-1), ((0, 0), (0, 0), (0, pad)))
    w_router_hi = w_router.astype(BF16)
    w_router_lo = (w_router - w_router_hi.astype(F32)).astype(BF16)
    w_router = jnp.concatenate([w_router_hi, w_router_lo], axis=-1)
    b_router = jnp.pad(jnp.concatenate([router_g_b, router_e_b], axis=-1), ((0, 0), (0, pad)))[:, None, :]
    vec3 = lambda v: v[:, None, :]
    p2 = p.reshape(depth, t, p.shape[-1])

    bias = _bias_tiles(rel_bias)
    xf = x.reshape(t, d)
    xb = xf.astype(BF16)
    for i in range(depth):
        qk = _proj_qk(xb, w_in, i, tn=attn_w)
        vt = _proj_vt(xb, w_in, i, batch, attn_w, col_block=2)
        u = _proj_glu(xb, w_in, i, a_col=3 * attn_w, width=conv_width)
        attn = _attention(qk, vt, bias, batch, n_heads)
        conv = _conv_module(u, conv_w, vec3(conv_b), vec3(conv_ln_g), vec3(conv_ln_b), i, batch)
        h, route_i, route_w, cnt = _outproj(attn, conv, w_out, xf, vec3(ln1_g), vec3(ln1_b),
                                            w_router, b_router, i, alpha)
        plan = _moe_plan(route_i, cnt, 2 * t)
        ys = _gmm(plan, h, expert_w_gu, expert_w_down, i)
        xf, xb = _ple(plan[0], h, route_w, ys, vec3(ln2_g), vec3(ln2_b), ple_gate_w, p2, ple_w, i, alpha)
    return xf.reshape(batch, seq, d)
```

```python
import functools
import math

import numpy as np
import jax
import jax.numpy as jnp
from jax import lax
from jax.experimental import pallas as pl
from jax.experimental.pallas import tpu as pltpu

F32 = jnp.float32
BF16 = jnp.bfloat16

HEAD_DIM = 128
BLOCK = 256
TOP_BLOCKS = 3
CONV_K = 31
N_BUCKETS = 32
MAX_DISTANCE = 128
N_GROUPS = 4
EXPERTS_PER_GROUP = 8
N_EXPERTS = N_GROUPS * EXPERTS_PER_GROUP
LN_EPS = 1e-5
LOG2_E = math.log2(math.e)
NEG = -1e30
ROUTER_LANES = 128
VMEM_LIMIT = 56 * 1024 * 1024


def _bucket_thresholds():
    n = np.arange(0, 4 * MAX_DISTANCE)
    max_exact = N_BUCKETS // 2
    nf = np.maximum(n, max_exact).astype(np.float32)
    large = max_exact + (np.log(nf / np.float32(max_exact)) / np.float32(math.log(MAX_DISTANCE / max_exact))
                         * np.float32(N_BUCKETS - max_exact)).astype(np.int32)
    bucket = np.where(n < max_exact, n, np.minimum(large, N_BUCKETS - 1))
    assert np.all(np.diff(bucket) >= 0) and bucket[-1] == N_BUCKETS - 1
    return [int(np.argmax(bucket >= b)) for b in range(N_BUCKETS)]


_BUCKET_START = _bucket_thresholds()


def _params(**kw):
    return pltpu.CompilerParams(vmem_limit_bytes=VMEM_LIMIT, **kw)


def _bias_kernel(rb_ref, out_ref):
    h = pl.program_id(0)
    kj = lax.broadcasted_iota(jnp.int32, (BLOCK, BLOCK), 0)
    qi = lax.broadcasted_iota(jnp.int32, (BLOCK, BLOCK), 1)
    for d in range(3):
        rel = qi - kj + d * BLOCK
        val = jnp.full((BLOCK, BLOCK), rb_ref[0, h], F32)
        for b in range(1, N_BUCKETS):
            val = jnp.where(rel >= _BUCKET_START[b], rb_ref[b, h], val)
        val = val * LOG2_E
        if d == 0:
            val = jnp.where(rel >= 0, val, NEG)
        out_ref[d] = val


def _bias_tiles(rel_bias):
    n_heads = rel_bias.shape[1]
    assert _BUCKET_START[-1] <= BLOCK + 1
    return pl.pallas_call(
        _bias_kernel,
        grid=(n_heads,),
        in_specs=[pl.BlockSpec(memory_space=pltpu.SMEM)],
        out_specs=pl.BlockSpec((None, 3, BLOCK, BLOCK), lambda h: (h, 0, 0, 0)),
        out_shape=jax.ShapeDtypeStruct((n_heads, 3, BLOCK, BLOCK), F32),
        name="t5_bias_tiles",
    )(rel_bias)


def _cast_weights_once(row_tile_id, pairs):
    @pl.when(row_tile_id == 0)
    def _():
        for src, dst in pairs:
            dst[...] = src[...].astype(BF16)


def _mm_kernel(x_ref, w_ref, o_ref, wb_ref):
    _cast_weights_once(pl.program_id(1), [(w_ref, wb_ref)])
    o_ref[...] = jnp.dot(x_ref[...], wb_ref[...], preferred_element_type=F32).astype(o_ref.dtype)


def _proj_qk(xb, w_in, layer, tm=512, tn=1024):
    t, d = xb.shape
    n_out = 2 * tn
    return pl.pallas_call(
        _mm_kernel,
        grid=(n_out // tn, t // tm),
        in_specs=[pl.BlockSpec((tm, d), lambda j, i: (i, 0)),
                  pl.BlockSpec((None, d, tn), lambda j, i: (layer, 0, j))],
        out_specs=pl.BlockSpec((tm, tn), lambda j, i: (i, j)),
        out_shape=jax.ShapeDtypeStruct((t, n_out), BF16),
        scratch_shapes=[pltpu.VMEM((d, tn), BF16)],
        compiler_params=_params(dimension_semantics=("arbitrary", "arbitrary")),
        name="proj_qk",
    )(xb, w_in)


def _proj_vt_kernel(x_ref, w_ref, o_ref, wb_ref):
    _cast_weights_once(pl.program_id(0), [(w_ref, wb_ref)])
    r = jnp.dot(x_ref[...], wb_ref[...], preferred_element_type=F32)
    o_ref[...] = r.T.astype(o_ref.dtype)


def _proj_vt(xb, w_in, layer, batch, width, col_block, tm=512):
    t, d = xb.shape
    seq = t // batch
    per_b = seq // tm
    return pl.pallas_call(
        _proj_vt_kernel,
        grid=(t // tm,),
        in_specs=[pl.BlockSpec((tm, d), lambda i: (i, 0)),
                  pl.BlockSpec((None, d, width), lambda i: (layer, 0, col_block))],
        out_specs=pl.BlockSpec((None, width, tm), lambda i: (i // per_b, 0, i % per_b)),
        out_shape=jax.ShapeDtypeStruct((batch, width, seq), BF16),
        scratch_shapes=[pltpu.VMEM((d, width), BF16)],
        compiler_params=_params(dimension_semantics=("arbitrary",)),
        name="proj_vt",
    )(xb, w_in)


def _proj_glu_kernel(x_ref, wa_ref, wg_ref, o_ref, wab_ref, wgb_ref):
    _cast_weights_once(pl.program_id(1), [(wa_ref, wab_ref), (wg_ref, wgb_ref)])
    x = x_ref[...]
    a = jnp.dot(x, wab_ref[...], preferred_element_type=F32)
    g = jnp.dot(x, wgb_ref[...], preferred_element_type=F32)
    o_ref[...] = a * jax.nn.sigmoid(g)


def _proj_glu(xb, w_in, layer, a_col, width, tm=512, tn=512):
    t, d = xb.shape
    a_blk = a_col // tn
    g_blk = (a_col + width) // tn
    return pl.pallas_call(
        _proj_glu_kernel,
        grid=(width // tn, t // tm),
        in_specs=[pl.BlockSpec((tm, d), lambda j, i: (i, 0)),
                  pl.BlockSpec((None, d, tn), lambda j, i: (layer, 0, a_blk + j)),
                  pl.BlockSpec((None, d, tn), lambda j, i: (layer, 0, g_blk + j))],
        out_specs=pl.BlockSpec((tm, tn), lambda j, i: (i, j)),
        out_shape=jax.ShapeDtypeStruct((t, width), F32),
        scratch_shapes=[pltpu.VMEM((d, tn), BF16), pltpu.VMEM((d, tn), BF16)],
        compiler_params=_params(dimension_semantics=("arbitrary", "arbitrary")),
        name="proj_glu",
    )(xb, w_in, w_in)


ATTN_HEADS_PER_STEP = 2
_NT = (((1,), (1,)), ((), ()))


def _block_penalty(kmean, q, c):
    gate = lax.dot_general(kmean, q.astype(F32), _NT, precision=lax.Precision.HIGHEST,
                           preferred_element_type=F32)
    row = lax.broadcasted_iota(jnp.int32, gate.shape, 0)
    gate_m = jnp.where(row < c, gate, -jnp.inf)
    pens = []
    for n in range(c):
        g_n = gate[n:n + 1, :]
        beats = (gate_m > g_n) | ((gate_m == g_n) & (row < n))
        rank = jnp.sum(beats.astype(F32), axis=0, keepdims=True)
        pens.append(jnp.where(rank < TOP_BLOCKS, 0.0, NEG))
    return pens


def _attn_row(c, hh, q_ref, k_ref, vt_ref, bias_ref, o_ref, kmean_ref, logit_ref, prob_ref, scale):
    hs = slice(hh * HEAD_DIM, (hh + 1) * HEAD_DIM)
    q = q_ref[:, hs]
    keys = (c + 1) * BLOCK
    groups = BLOCK // SUBLANES
    pens = _block_penalty(kmean_ref[hh], q, c) if c > TOP_BLOCKS else None
    m8 = None
    for n in range(c + 1):
        s = lax.dot_general(k_ref[n * BLOCK:(n + 1) * BLOCK, hs], q, _NT, preferred_element_type=F32)
        b = s * scale + bias_ref[hh, min(c - n, 2)]
        if pens is not None and n < c:
            b = b + pens[n]
        logit_ref[hh, n] = b
        bm = jnp.max(b.reshape(groups, SUBLANES, BLOCK), axis=0)
        m8 = bm if m8 is None else jnp.maximum(m8, bm)
    m = jnp.max(m8, axis=0, keepdims=True)
    l8 = jnp.zeros((SUBLANES, BLOCK), F32)
    for n in range(c + 1):
        p = jnp.exp2(logit_ref[hh, n] - m)
        l8 = l8 + jnp.sum(p.reshape(groups, SUBLANES, BLOCK), axis=0)
        prob_ref[hh, n * BLOCK:(n + 1) * BLOCK, :] = p.astype(BF16)
    l = jnp.sum(l8, axis=0, keepdims=True)
    acc = jnp.dot(vt_ref[hs, 0:keys], prob_ref[hh, 0:keys, :], preferred_element_type=F32)
    o_ref[:, hs] = (acc / l).T.astype(o_ref.dtype)


def _attn_kernel(q_ref, k_ref, vt_ref, bias_ref, o_ref, kmean_ref, logit_ref, prob_ref, *, n_blocks, scale):
    c_id = pl.program_id(2)
    heads = q_ref.shape[1] // HEAD_DIM

    @pl.when(c_id == 0)
    def _():
        for hh in range(heads):
            for n in range(n_blocks):
                kb = k_ref[n * BLOCK:(n + 1) * BLOCK, hh * HEAD_DIM:(hh + 1) * HEAD_DIM]
                kmean_ref[hh, n:n + 1, :] = jnp.mean(kb.astype(F32), axis=0, keepdims=True)

    for c in range(n_blocks):
        @pl.when(c_id == c)
        def _(c=c):
            for hh in range(heads):
                _attn_row(c, hh, q_ref, k_ref, vt_ref, bias_ref, o_ref, kmean_ref, logit_ref, prob_ref, scale)


def _attention(qk, vt, bias, batch, n_heads):
    t = qk.shape[0]
    seq = t // batch
    nb = seq // BLOCK
    hp = ATTN_HEADS_PER_STEP
    w = hp * HEAD_DIM
    kern = functools.partial(_attn_kernel, n_blocks=nb, scale=HEAD_DIM ** -0.5 * LOG2_E)
    return pl.pallas_call(
        kern,
        grid=(batch, n_heads // hp, nb),
        in_specs=[pl.BlockSpec((BLOCK, w), lambda b, h, c: (b * nb + c, h)),
                  pl.BlockSpec((seq, w), lambda b, h, c: (b, n_heads // hp + h)),
                  pl.BlockSpec((None, w, seq), lambda b, h, c: (b, h, 0)),
                  pl.BlockSpec((hp, 3, BLOCK, BLOCK), lambda b, h, c: (h, 0, 0, 0))],
        out_specs=pl.BlockSpec((BLOCK, w), lambda b, h, c: (b * nb + c, h)),
        out_shape=jax.ShapeDtypeStruct((t, n_heads * HEAD_DIM), BF16),
        scratch_shapes=[pltpu.VMEM((hp, nb, HEAD_DIM), F32),
                        pltpu.VMEM((hp, nb, BLOCK, BLOCK), F32),
                        pltpu.VMEM((hp, seq, BLOCK), BF16)],
        compiler_params=_params(),
        name="moba_attention",
    )(qk, qk, vt, bias)


CONV_TILE = 256
CONV_HALO = 32
CONV_ROWS = 64
LANES = 128
SUBLANES = 8


def _conv_kernel(prev_ref, cur_ref, w_ref, b_ref, g_ref, beta_ref, o_ref, win_ref, y_ref):
    s = pl.program_id(1)
    win_ref[0:CONV_HALO, :] = jnp.where(s > 0, prev_ref[...], 0.0)
    win_ref[CONV_HALO:, :] = cur_ref[...]
    width = cur_ref.shape[1]
    first = CONV_HALO - (CONV_K - 1)
    phases = [[k for k in range(CONV_K) if (first + k) % SUBLANES == a] for a in range(SUBLANES)]
    for lc in range(width // LANES):
        ls = slice(lc * LANES, (lc + 1) * LANES)
        for rc in range(CONV_TILE // CONV_ROWS):
            r0 = rc * CONV_ROWS
            acc = jnp.broadcast_to(b_ref[:, ls], (CONV_ROWS, LANES))
            for a, taps in enumerate(phases):
                rows = CONV_ROWS + (SUBLANES if a else 0)
                z = None
                for k in taps:
                    start = r0 + first + k - a
                    term = win_ref[start:start + rows, ls] * w_ref[k:k + 1, ls]
                    z = term if z is None else z + term
                acc = acc + z[a:a + CONV_ROWS, :]
            y_ref[r0:r0 + CONV_ROWS, ls] = acc
    y = y_ref[...]
    mu = jnp.mean(y, axis=-1, keepdims=True)
    yc = y - mu
    var = jnp.mean(yc * yc, axis=-1, keepdims=True)
    z = yc * lax.rsqrt(var + LN_EPS) * g_ref[...] + beta_ref[...]
    o_ref[...] = (z * jax.nn.sigmoid(z)).astype(o_ref.dtype)


def _conv_module(u, conv_w, conv_b, ln_g, ln_b, layer, batch):
    t, width = u.shape
    seq = t // batch
    tiles = seq // CONV_TILE
    halo_per_tile = CONV_TILE // CONV_HALO
    vec = pl.BlockSpec((None, 1, width), lambda b, s: (layer, 0, 0))
    return pl.pallas_call(
        _conv_kernel,
        grid=(batch, tiles),
        in_specs=[pl.BlockSpec((CONV_HALO, width),
                               lambda b, s: (jnp.maximum((b * tiles + s) * halo_per_tile - 1, 0), 0)),
                  pl.BlockSpec((CONV_TILE, width), lambda b, s: (b * tiles + s, 0)),
                  pl.BlockSpec((None, CONV_K, width), lambda b, s: (layer, 0, 0)),
                  vec, vec, vec],
        out_specs=pl.BlockSpec((CONV_TILE, width), lambda b, s: (b * tiles + s, 0)),
        out_shape=jax.ShapeDtypeStruct((t, width), BF16),
        scratch_shapes=[pltpu.VMEM((CONV_HALO + CONV_TILE, width), F32), pltpu.VMEM((CONV_TILE, width), F32)],
        compiler_params=_params(),
        name="conformer_conv",
    )(u, u, conv_w, conv_b, ln_g, ln_b)


def _layer_norm(y, g, b):
    mu = jnp.mean(y, axis=-1, keepdims=True)
    yc = y - mu
    var = jnp.mean(yc * yc, axis=-1, keepdims=True)
    return yc * lax.rsqrt(var + LN_EPS) * g + b


def _route(logits):
    lane = lax.broadcasted_iota(jnp.int32, logits.shape, 1)
    is_group = lane < N_GROUPS
    gl = jnp.where(is_group, logits, -jnp.inf)
    gmax = jnp.max(gl, axis=-1, keepdims=True)
    g_top = jnp.min(jnp.where(gl == gmax, lane, ROUTER_LANES), axis=-1, keepdims=True)
    p_group = 1.0 / jnp.sum(jnp.where(is_group, jnp.exp(gl - gmax), 0.0), axis=-1, keepdims=True)
    e_idx = lane - N_GROUPS
    in_group = (e_idx >= 0) & (e_idx < N_EXPERTS) & ((e_idx // EXPERTS_PER_GROUP) == g_top)
    el = jnp.where(in_group, logits, -jnp.inf)
    m1 = jnp.max(el, axis=-1, keepdims=True)
    i1 = jnp.min(jnp.where(el == m1, lane, ROUTER_LANES), axis=-1, keepdims=True)
    el2 = jnp.where(lane == i1, -jnp.inf, el)
    m2 = jnp.max(el2, axis=-1, keepdims=True)
    i2 = jnp.min(jnp.where(el2 == m2, lane, ROUTER_LANES), axis=-1, keepdims=True)
    e2 = jnp.exp(m2 - m1)
    w1 = p_group / (1.0 + e2)
    w2 = p_group * e2 / (1.0 + e2)
    return i1, i2, w1, w2


def _outproj_kernel(attn_ref, conv_ref, w_ref, x_ref, g_ref, b_ref, wr_ref, br_ref,
                    h_ref, ri_ref, rw_ref, cnt_ref, carry_ref, wb_ref, *, alpha):
    i = pl.program_id(0)
    _cast_weights_once(i, [(w_ref, wb_ref)])

    @pl.when(i == 0)
    def _():
        carry_ref[...] = jnp.zeros_like(carry_ref)

    ka = attn_ref.shape[1]
    mix = jnp.dot(attn_ref[...], wb_ref[0:ka, :], preferred_element_type=F32)
    mix = mix + jnp.dot(conv_ref[...], wb_ref[ka:, :], preferred_element_type=F32)
    h = _layer_norm(alpha * x_ref[...] + mix, g_ref[...], b_ref[...])
    _store_row_chunks(h_ref, h)
    h_hi = h.astype(BF16)
    h_lo = (h - h_hi.astype(F32)).astype(BF16)
    t_hi = jnp.dot(h_hi, wr_ref[...], preferred_element_type=F32)
    t_lo = jnp.dot(h_lo, wr_ref[:, 0:ROUTER_LANES], preferred_element_type=F32)
    logits = t_hi[:, 0:ROUTER_LANES] + t_hi[:, ROUTER_LANES:] + t_lo + br_ref[...]
    i1, i2, w1, w2 = _route(logits)

    tm = logits.shape[0]
    lane = lax.broadcasted_iota(jnp.int32, logits.shape, 1)
    onehot = jnp.where((lane == i1) | (lane == i2), 1.0, 0.0)
    rr = lax.broadcasted_iota(jnp.int32, (tm, tm), 0)
    cc = lax.broadcasted_iota(jnp.int32, (tm, tm), 1)
    earlier = jnp.where(cc < rr, 1.0, 0.0).astype(BF16)
    before = jnp.dot(earlier, onehot.astype(BF16), preferred_element_type=F32) + carry_ref[...]
    r1 = jnp.sum(jnp.where(lane == i1, before, 0.0), axis=-1, keepdims=True)
    r2 = jnp.sum(jnp.where(lane == i2, before, 0.0), axis=-1, keepdims=True)
    carry_ref[...] += jnp.sum(onehot, axis=0, keepdims=True)
    cnt_ref[...] = jnp.broadcast_to(carry_ref[...], cnt_ref.shape)
    rec = jnp.where(lane == 0, (i1 - N_GROUPS).astype(F32),
                    jnp.where(lane == 1, (i2 - N_GROUPS).astype(F32),
                              jnp.where(lane == 2, r1, jnp.where(lane == 3, r2, 0.0))))
    ri_ref[...] = rec.T[0:ri_ref.shape[0], :].astype(jnp.int32)
    rw_ref[...] = jnp.where(lane == 0, w1, jnp.where(lane == 1, w2, 0.0))


def _outproj(attn, conv, w_out, x, ln_g, ln_b, w_router, b_router, layer, alpha, tm=256):
    t, d = x.shape
    ka, kc = attn.shape[1], conv.shape[1]
    vec = pl.BlockSpec((None, 1, d), lambda i: (layer, 0, 0))
    row = lambda w: pl.BlockSpec((tm, w), lambda i: (i, 0))
    return pl.pallas_call(
        functools.partial(_outproj_kernel, alpha=alpha),
        grid=(t // tm,),
        in_specs=[row(ka), row(kc),
                  pl.BlockSpec((None, ka + kc, d), lambda i: (layer, 0, 0), pipeline_mode=pl.Buffered(1)),
                  row(d), vec, vec,
                  pl.BlockSpec((None, d, 2 * ROUTER_LANES), lambda i: (layer, 0, 0)),
                  pl.BlockSpec((None, 1, ROUTER_LANES), lambda i: (layer, 0, 0))],
        out_specs=[pl.BlockSpec((tm * d // LANES, LANES), lambda i: (i, 0)),
                   pl.BlockSpec((SUBLANES, tm), lambda i: (0, i)), row(ROUTER_LANES),
                   pl.BlockSpec((8, ROUTER_LANES), lambda i: (0, 0))],
        out_shape=[jax.ShapeDtypeStruct((t * d // LANES, LANES), F32),
                   jax.ShapeDtypeStruct((SUBLANES, t), jnp.int32),
                   jax.ShapeDtypeStruct((t, ROUTER_LANES), F32),
                   jax.ShapeDtypeStruct((8, ROUTER_LANES), F32)],
        scratch_shapes=[pltpu.VMEM((1, ROUTER_LANES), F32), pltpu.VMEM((ka + kc, d), BF16)],
        compiler_params=_params(dimension_semantics=("arbitrary",)),
        name="outproj_ln_router",
    )(attn, conv, w_out, x, ln_g, ln_b, w_router, b_router)


MOE_TILE = 256


def _moe_plan(route_i, cnt, n_rows):
    e = route_i[0:2, :]
    rank = route_i[2:4, :]
    counts = cnt[0, N_GROUPS:N_GROUPS + N_EXPERTS].astype(jnp.int32)
    ends = jnp.cumsum(counts)
    offs = ends - counts
    start = functools.reduce(jnp.add, [jnp.where(e == k, offs[k], 0) for k in range(N_EXPERTS)])
    pos = (start + rank).reshape(-1)
    n_tiles = n_rows // MOE_TILE
    first_tile = offs // MOE_TILE
    n_items_e = jnp.where(counts > 0, (ends - 1) // MOE_TILE - first_tile + 1, 0)
    item_end = jnp.cumsum(n_items_e)
    item_start = item_end - n_items_e
    n_items = n_tiles + N_EXPERTS - 1
    j = jnp.arange(n_items, dtype=jnp.int32)
    valid = j < item_end[-1]
    jj = jnp.clip(j, 0, jnp.maximum(item_end[-1] - 1, 0))
    ex = jnp.minimum(jnp.sum((jj[:, None] >= item_end[None, :]).astype(jnp.int32), axis=1), N_EXPERTS - 1)
    tile = first_tile[ex] + (jj - item_start[ex])
    lo = jnp.where(valid, jnp.maximum(offs[ex], tile * MOE_TILE) - tile * MOE_TILE, 0)
    hi = jnp.where(valid, jnp.minimum(ends[ex], (tile + 1) * MOE_TILE) - tile * MOE_TILE, 0)
    return pos, tile.astype(jnp.int32), ex.astype(jnp.int32), lo.astype(jnp.int32), hi.astype(jnp.int32)


def _pack_bf16_pair(y):
    n = y.shape[1] // 2
    hi = lax.bitcast_convert_type(y[:, :n].astype(BF16).astype(F32), jnp.uint32)
    lo = lax.bitcast_convert_type(y[:, n:].astype(BF16).astype(F32), jnp.uint32)
    return hi | (lo >> 16)


def _unpack_bf16_pair(u):
    hi = lax.bitcast_convert_type(u & jnp.uint32(0xFFFF0000), F32)
    lo = lax.bitcast_convert_type(u << 16, F32)
    return hi, lo


def _store_row_chunks(ref, val):
    rows, width = val.shape
    n = width // LANES
    for c in range(n):
        ref[pl.ds(c, rows, stride=n), :] = val[:, c * LANES:(c + 1) * LANES]


def _start_alternating(copies):
    for k, cp in enumerate(copies):
        cp.start(priority=k % 2)


def _load_row_chunks(ref, rows, n):
    return jnp.concatenate([ref[pl.ds(c, rows, stride=n), :] for c in range(n)], axis=1)


def _gmm_kernel(pos_ref, tile_ref, exp_ref, lo_ref, hi_ref, h_ref, wgu_ref, wd_ref, ys_ref,
                inv_ref, xbuf, acc_ref, wgu_b_ref, wd_b_ref, sems):
    j = pl.program_id(0)
    n_items = pl.num_programs(0)
    tm = acc_ref.shape[0]
    chunks = acc_ref.shape[1] // LANES
    tokens = pos_ref.shape[0] // 2
    cur = tile_ref[j]
    nxt = tile_ref[jnp.minimum(j + 1, n_items - 1)]
    first = (j == 0) | (cur != tile_ref[jnp.maximum(j - 1, 0)])
    last = (j == n_items - 1) | (nxt != cur)

    def row_copies(tile):
        slot = tile % 2
        return [pltpu.make_async_copy(h_ref.at[pl.ds(inv_ref[tile * tm + r] * chunks, chunks), :],
                                      xbuf.at[slot, pl.ds(r * chunks, chunks), :], sems.at[slot])
                for r in range(tm)]

    @pl.when(j == 0)
    def _():
        def fill(t, carry):
            inv_ref[pos_ref[t]] = t
            inv_ref[pos_ref[tokens + t]] = t
            return carry
        lax.fori_loop(0, tokens, fill, 0, unroll=8)
        for cp in row_copies(cur):
            cp.start()

    @pl.when((j < n_items - 1) & (nxt != cur))
    def _():
        for cp in row_copies(nxt):
            cp.start()

    @pl.when(first)
    def _():
        for cp in row_copies(cur):
            cp.wait()

    @pl.when((j == 0) | (exp_ref[j] != exp_ref[jnp.maximum(j - 1, 0)]))
    def _():
        wgu_b_ref[...] = wgu_ref[...].astype(BF16)
        wd_b_ref[...] = wd_ref[...].astype(BF16)

    f = wd_ref.shape[0]
    x = _load_row_chunks(xbuf.at[cur % 2], tm, chunks).astype(BF16)
    hg = jnp.dot(x, wgu_b_ref[...], preferred_element_type=F32)
    gate = hg[:, :f]
    row = lax.broadcasted_iota(jnp.int32, gate.shape, 0)
    mine = (row >= lo_ref[j]) & (row < hi_ref[j])
    act = jnp.where(mine, gate * jax.nn.sigmoid(gate) * hg[:, f:], 0.0)
    y = jnp.dot(act.astype(BF16), wd_b_ref[...], preferred_element_type=F32)

    @pl.when(first)
    def _():
        acc_ref[...] = y

    @pl.when(jnp.logical_not(first))
    def _():
        acc_ref[...] += y

    @pl.when(last)
    def _():
        _store_row_chunks(ys_ref, _pack_bf16_pair(acc_ref[...]))


def _gmm(plan, h_chunks, w_gu, w_down, layer):
    pos, tile, ex, lo, hi = plan
    n_rows = pos.shape[0]
    d = w_gu.shape[2]
    f2 = w_gu.shape[-1]
    f = w_down.shape[2]
    chunks = d // LANES
    return pl.pallas_call(
        _gmm_kernel,
        grid_spec=pltpu.PrefetchScalarGridSpec(
            num_scalar_prefetch=5,
            grid=(tile.shape[0],),
            in_specs=[pl.BlockSpec(memory_space=pl.ANY),
                      pl.BlockSpec((None, None, d, f2), lambda j, pos, tile, ex, lo, hi: (layer, ex[j], 0, 0)),
                      pl.BlockSpec((None, None, f, d), lambda j, pos, tile, ex, lo, hi: (layer, ex[j], 0, 0))],
            out_specs=pl.BlockSpec((MOE_TILE * chunks // 2, LANES),
                                   lambda j, pos, tile, ex, lo, hi: (tile[j], 0)),
            scratch_shapes=[pltpu.SMEM((n_rows,), jnp.int32),
                            pltpu.VMEM((2, MOE_TILE * chunks, LANES), F32),
                            pltpu.VMEM((MOE_TILE, d), F32),
                            pltpu.VMEM((d, f2), BF16),
                            pltpu.VMEM((f, d), BF16),
                            pltpu.SemaphoreType.DMA((2,))]),
        out_shape=jax.ShapeDtypeStruct((n_rows * chunks // 2, LANES), jnp.uint32),
        compiler_params=_params(dimension_semantics=("arbitrary",)),
        name="moe_grouped_matmul",
    )(pos, tile, ex, lo, hi, h_chunks, w_gu, w_down)


def _ple_kernel(pos_ref, h_ref, rw_ref, ys_ref, g_ref, b_ref, wgate_ref, p_ref, wp_ref, x_ref, xb_ref,
                ybuf, sems, wgate_b_ref, wp_b_ref, *, alpha):
    i = pl.program_id(0)
    n_tiles = pl.num_programs(0)
    tm = rw_ref.shape[0]
    tokens = pos_ref.shape[0] // 2
    yc = ybuf.shape[2] // tm
    hc = h_ref.shape[0] // tm

    def row_copies(tile, slot):
        return [pltpu.make_async_copy(ys_ref.at[pl.ds(pos_ref[s * tokens + tile * tm + r] * yc, yc), :],
                                      ybuf.at[slot, s, pl.ds(r * yc, yc), :], sems.at[slot])
                for r in range(tm) for s in range(2)]

    @pl.when(i == 0)
    def _():
        wgate_b_ref[...] = wgate_ref[...].astype(BF16)
        wp_b_ref[...] = wp_ref[...].astype(BF16)
        _start_alternating(row_copies(0, 0))

    slot = i % 2
    ahead = jnp.minimum(i + 1, n_tiles - 1)
    _start_alternating(row_copies(ahead, 1 - slot))
    for cp in row_copies(i, slot):
        cp.wait()
    rw = rw_ref[...]
    y0_hi, y0_lo = _unpack_bf16_pair(_load_row_chunks(ybuf.at[slot, 0], tm, yc))
    y1_hi, y1_lo = _unpack_bf16_pair(_load_row_chunks(ybuf.at[slot, 1], tm, yc))
    w0, w1 = rw[:, 0:1], rw[:, 1:2]
    f = jnp.concatenate([w0 * y0_hi + w1 * y1_hi, w0 * y0_lo + w1 * y1_lo], axis=1)
    h2 = _layer_norm(alpha * _load_row_chunks(h_ref, tm, hc) + f, g_ref[...], b_ref[...])
    gate = jax.nn.sigmoid(jnp.dot(h2.astype(BF16), wgate_b_ref[...], preferred_element_type=F32))
    pe = jnp.dot(p_ref[...].astype(BF16), wp_b_ref[...], preferred_element_type=F32)
    x_new = h2 + gate * pe
    x_ref[...] = x_new
    xb_ref[...] = x_new.astype(BF16)

    @pl.when(i == n_tiles - 1)
    def _():
        for cp in row_copies(ahead, 1 - slot):
            cp.wait()


def _ple(pos, h_chunks, rw, ys, ln_g, ln_b, w_gate, p, w_ple, layer, alpha, tm=256):
    d = w_gate.shape[-1]
    t = h_chunks.shape[0] * LANES // d
    pd = p.shape[-1]
    vec = pl.BlockSpec((None, 1, d), lambda i, pos: (layer, 0, 0))
    row = lambda w: pl.BlockSpec((tm, w), lambda i, pos: (i, 0))
    return pl.pallas_call(
        functools.partial(_ple_kernel, alpha=alpha),
        grid_spec=pltpu.PrefetchScalarGridSpec(
            num_scalar_prefetch=1,
            grid=(t // tm,),
            in_specs=[pl.BlockSpec((tm * d // LANES, LANES), lambda i, pos: (i, 0)),
                      row(ROUTER_LANES), pl.BlockSpec(memory_space=pl.ANY), vec, vec,
                      pl.BlockSpec((None, d, d), lambda i, pos: (layer, 0, 0), pipeline_mode=pl.Buffered(1)),
                      pl.BlockSpec((None, tm, pd), lambda i, pos: (layer, i, 0)),
                      pl.BlockSpec((None, pd, d), lambda i, pos: (layer, 0, 0), pipeline_mode=pl.Buffered(1))],
            out_specs=[row(d), row(d)],
            scratch_shapes=[pltpu.VMEM((2, 2, tm * d // (2 * LANES), LANES), jnp.uint32),
                            pltpu.SemaphoreType.DMA((2,)),
                            pltpu.VMEM((d, d), BF16), pltpu.VMEM((pd, d), BF16)]),
        out_shape=[jax.ShapeDtypeStruct((t, d), F32), jax.ShapeDtypeStruct((t, d), BF16)],
        compiler_params=_params(dimension_semantics=("arbitrary",)),
        name="gather_ln2_ple",
    )(pos, h_chunks, rw, ys, ln_g, ln_b, w_gate, p, w_ple)


def kernel(x, p, w_in, conv_w, conv_b, conv_ln_g, conv_ln_b, w_out, rel_bias, ln1_g, ln1_b, router_g_w, router_g_b, router_e_w, router_e_b, expert_w_gu, expert_w_down, ln2_g, ln2_b, ple_w, ple_gate_w):
    batch, seq, d = x.shape
    depth = w_in.shape[0]
    n_heads = rel_bias.shape[1]
    attn_w = n_heads * HEAD_DIM
    conv_width = conv_w.shape[2]
    t = batch * seq
    alpha = (2 * depth) ** 0.25

    pad = ROUTER_LANES - N_GROUPS - N_EXPERTS
    w_router = jnp.pad(jnp.concatenate([router_g_w, router_e_w], axis=-1), ((0, 0), (0, 0), (0, pad)))
    w_router_hi = w_router.astype(BF16)
    w_router_lo = (w_router - w_router_hi.astype(F32)).astype(BF16)
    w_router = jnp.concatenate([w_router_hi, w_router_lo], axis=-1)
    b_router = jnp.pad(jnp.concatenate([router_g_b, router_e_b], axis=-1), ((0, 0), (0, pad)))[:, None, :]
    vec3 = lambda v: v[:, None, :]
    p2 = p.reshape(depth, t, p.shape[-1])

    bias = _bias_tiles(rel_bias)
    xf = x.reshape(t, d)
    xb = xf.astype(BF16)
    for i in range(depth):
        qk = _proj_qk(xb, w_in, i, tn=attn_w)
        vt = _proj_vt(xb, w_in, i, batch, attn_w, col_block=2)
        u = _proj_glu(xb, w_in, i, a_col=3 * attn_w, width=conv_width)
        attn = _attention(qk, vt, bias, batch, n_heads)
        conv = _conv_module(u, conv_w, vec3(conv_b), vec3(conv_ln_g), vec3(conv_ln_b), i, batch)
        h, route_i, route_w, cnt = _outproj(attn, conv, w_out, xf, vec3(ln1_g), vec3(ln1_b),
                                            w_router, b_router, i, alpha)
        plan = _moe_plan(route_i, cnt, 2 * t)
        ys = _gmm(plan, h, expert_w_gu, expert_w_down, i)
        xf, xb = _ple(plan[0], h, route_w, ys, vec3(ln2_g), vec3(ln2_b), ple_gate_w, p2, ple_w, i, alpha)
    return xf.reshape(batch, seq, d)
```

```python
import functools
import math

import numpy as np
import jax
import jax.numpy as jnp
from jax import lax
from jax.experimental import pallas as pl
from jax.experimental.pallas import tpu as pltpu

F32 = jnp.float32
BF16 = jnp.bfloat16

HEAD_DIM = 128
BLOCK = 256
TOP_BLOCKS = 3
CONV_K = 31
N_BUCKETS = 32
MAX_DISTANCE = 128
N_GROUPS = 4
EXPERTS_PER_GROUP = 8
N_EXPERTS = N_GROUPS * EXPERTS_PER_GROUP
LN_EPS = 1e-5
LOG2_E = math.log2(math.e)
NEG = -1e30
ROUTER_LANES = 128
VMEM_LIMIT = 56 * 1024 * 1024


def _bucket_thresholds():
    n = np.arange(0, 4 * MAX_DISTANCE)
    max_exact = N_BUCKETS // 2
    nf = np.maximum(n, max_exact).astype(np.float32)
    large = max_exact + (np.log(nf / np.float32(max_exact)) / np.float32(math.log(MAX_DISTANCE / max_exact))
                         * np.float32(N_BUCKETS - max_exact)).astype(np.int32)
    bucket = np.where(n < max_exact, n, np.minimum(large, N_BUCKETS - 1))
    assert np.all(np.diff(bucket) >= 0) and bucket[-1] == N_BUCKETS - 1
    return [int(np.argmax(bucket >= b)) for b in range(N_BUCKETS)]


_BUCKET_START = _bucket_thresholds()


def _params(**kw):
    return pltpu.CompilerParams(vmem_limit_bytes=VMEM_LIMIT, **kw)


def _bias_kernel(rb_ref, out_ref):
    h = pl.program_id(0)
    kj = lax.broadcasted_iota(jnp.int32, (BLOCK, BLOCK), 0)
    qi = lax.broadcasted_iota(jnp.int32, (BLOCK, BLOCK), 1)
    for d in range(3):
        rel = qi - kj + d * BLOCK
        val = jnp.full((BLOCK, BLOCK), rb_ref[0, h], F32)
        for b in range(1, N_BUCKETS):
            val = jnp.where(rel >= _BUCKET_START[b], rb_ref[b, h], val)
        val = val * LOG2_E
        if d == 0:
            val = jnp.where(rel >= 0, val, NEG)
        out_ref[d] = val


def _bias_tiles(rel_bias):
    n_heads = rel_bias.shape[1]
    assert _BUCKET_START[-1] <= BLOCK + 1
    return pl.pallas_call(
        _bias_kernel,
        grid=(n_heads,),
        in_specs=[pl.BlockSpec(memory_space=pltpu.SMEM)],
        out_specs=pl.BlockSpec((None, 3, BLOCK, BLOCK), lambda h: (h, 0, 0, 0)),
        out_shape=jax.ShapeDtypeStruct((n_heads, 3, BLOCK, BLOCK), F32),
        name="t5_bias_tiles",
    )(rel_bias)


def _cast_weights_once(row_tile_id, pairs):
    @pl.when(row_tile_id == 0)
    def _():
        for src, dst in pairs:
            dst[...] = src[...].astype(BF16)


def _mm_kernel(x_ref, w_ref, o_ref, wb_ref):
    _cast_weights_once(pl.program_id(1), [(w_ref, wb_ref)])
    o_ref[...] = jnp.dot(x_ref[...], wb_ref[...], preferred_element_type=F32).astype(o_ref.dtype)


def _proj_qk(xb, w_in, layer, tm=512, tn=1024):
    t, d = xb.shape
    n_out = 2 * tn
    return pl.pallas_call(
        _mm_kernel,
        grid=(n_out // tn, t // tm),
        in_specs=[pl.BlockSpec((tm, d), lambda j, i: (i, 0)),
                  pl.BlockSpec((None, d, tn), lambda j, i: (layer, 0, j))],
        out_specs=pl.BlockSpec((tm, tn), lambda j, i: (i, j)),
        out_shape=jax.ShapeDtypeStruct((t, n_out), BF16),
        scratch_shapes=[pltpu.VMEM((d, tn), BF16)],
        compiler_params=_params(dimension_semantics=("arbitrary", "arbitrary")),
        name="proj_qk",
    )(xb, w_in)


def _proj_vt_kernel(x_ref, w_ref, o_ref, wb_ref):
    _cast_weights_once(pl.program_id(0), [(w_ref, wb_ref)])
    r = jnp.dot(x_ref[...], wb_ref[...], preferred_element_type=F32)
    o_ref[...] = r.T.astype(o_ref.dtype)


def _proj_vt(xb, w_in, layer, batch, width, col_block, tm=512):
    t, d = xb.shape
    seq = t // batch
    per_b = seq // tm
    return pl.pallas_call(
        _proj_vt_kernel,
        grid=(t // tm,),
        in_specs=[pl.BlockSpec((tm, d), lambda i: (i, 0)),
                  pl.BlockSpec((None, d, width), lambda i: (layer, 0, col_block))],
        out_specs=pl.BlockSpec((None, width, tm), lambda i: (i // per_b, 0, i % per_b)),
        out_shape=jax.ShapeDtypeStruct((batch, width, seq), BF16),
        scratch_shapes=[pltpu.VMEM((d, width), BF16)],
        compiler_params=_params(dimension_semantics=("arbitrary",)),
        name="proj_vt",
    )(xb, w_in)


def _proj_glu_kernel(x_ref, wa_ref, wg_ref, o_ref, wab_ref, wgb_ref):
    _cast_weights_once(pl.program_id(1), [(wa_ref, wab_ref), (wg_ref, wgb_ref)])
    x = x_ref[...]
    a = jnp.dot(x, wab_ref[...], preferred_element_type=F32)
    g = jnp.dot(x, wgb_ref[...], preferred_element_type=F32)
    o_ref[...] = a * jax.nn.sigmoid(g)


def _proj_glu(xb, w_in, layer, a_col, width, tm=512, tn=512):
    t, d = xb.shape
    a_blk = a_col // tn
    g_blk = (a_col + width) // tn
    return pl.pallas_call(
        _proj_glu_kernel,
        grid=(width // tn, t // tm),
        in_specs=[pl.BlockSpec((tm, d), lambda j, i: (i, 0)),
                  pl.BlockSpec((None, d, tn), lambda j, i: (layer, 0, a_blk + j)),
                  pl.BlockSpec((None, d, tn), lambda j, i: (layer, 0, g_blk + j))],
        out_specs=pl.BlockSpec((tm, tn), lambda j, i: (i, j)),
        out_shape=jax.ShapeDtypeStruct((t, width), F32),
        scratch_shapes=[pltpu.VMEM((d, tn), BF16), pltpu.VMEM((d, tn), BF16)],
        compiler_params=_params(dimension_semantics=("arbitrary", "arbitrary")),
        name="proj_glu",
    )(xb, w_in, w_in)


ATTN_HEADS_PER_STEP = 4
_NT = (((1,), (1,)), ((), ()))


def _block_penalty(kmean, q, c):
    gate = lax.dot_general(kmean, q.astype(F32), _NT, precision=lax.Precision.HIGHEST,
                           preferred_element_type=F32)
    row = lax.broadcasted_iota(jnp.int32, gate.shape, 0)
    gate_m = jnp.where(row < c, gate, -jnp.inf)
    pens = []
    for n in range(c):
        g_n = gate[n:n + 1, :]
        beats = (gate_m > g_n) | ((gate_m == g_n) & (row < n))
        rank = jnp.sum(beats.astype(F32), axis=0, keepdims=True)
        pens.append(jnp.where(rank < TOP_BLOCKS, 0.0, NEG))
    return pens


def _attn_row(c, hh, q_ref, k_ref, vt_ref, bias_ref, o_ref, kmean_ref, logit_ref, prob_ref, scale):
    hs = slice(hh * HEAD_DIM, (hh + 1) * HEAD_DIM)
    q = q_ref[:, hs]
    keys = (c + 1) * BLOCK
    groups = BLOCK // SUBLANES
    pens = _block_penalty(kmean_ref[hh], q, c) if c > TOP_BLOCKS else None
    m8 = None
    for n in range(c + 1):
        s = lax.dot_general(k_ref[n * BLOCK:(n + 1) * BLOCK, hs], q, _NT, preferred_element_type=F32)
        b = s * scale + bias_ref[hh, min(c - n, 2)]
        if pens is not None and n < c:
            b = b + pens[n]
        logit_ref[hh, n] = b
        bm = jnp.max(b.reshape(groups, SUBLANES, BLOCK), axis=0)
        m8 = bm if m8 is None else jnp.maximum(m8, bm)
    m = jnp.max(m8, axis=0, keepdims=True)
    l8 = jnp.zeros((SUBLANES, BLOCK), F32)
    for n in range(c + 1):
        p = jnp.exp2(logit_ref[hh, n] - m)
        l8 = l8 + jnp.sum(p.reshape(groups, SUBLANES, BLOCK), axis=0)
        prob_ref[hh, n * BLOCK:(n + 1) * BLOCK, :] = p.astype(BF16)
    l = jnp.sum(l8, axis=0, keepdims=True)
    acc = jnp.dot(vt_ref[hs, 0:keys], prob_ref[hh, 0:keys, :], preferred_element_type=F32)
    o_ref[:, hs] = (acc / l).T.astype(o_ref.dtype)


def _attn_kernel(q_ref, k_ref, vt_ref, bias_ref, o_ref, kmean_ref, logit_ref, prob_ref, *, n_blocks, scale):
    c_id = pl.program_id(2)
    heads = q_ref.shape[1] // HEAD_DIM

    @pl.when(c_id == 0)
    def _():
        for hh in range(heads):
            for n in range(n_blocks):
                kb = k_ref[n * BLOCK:(n + 1) * BLOCK, hh * HEAD_DIM:(hh + 1) * HEAD_DIM]
                kmean_ref[hh, n:n + 1, :] = jnp.mean(kb.astype(F32), axis=0, keepdims=True)

    for c in range(n_blocks):
        @pl.when(c_id == c)
        def _(c=c):
            for hh in range(heads):
                _attn_row(c, hh, q_ref, k_ref, vt_ref, bias_ref, o_ref, kmean_ref, logit_ref, prob_ref, scale)


def _attention(qk, vt, bias, batch, n_heads):
    t = qk.shape[0]
    seq = t // batch
    nb = seq // BLOCK
    hp = ATTN_HEADS_PER_STEP
    w = hp * HEAD_DIM
    kern = functools.partial(_attn_kernel, n_blocks=nb, scale=HEAD_DIM ** -0.5 * LOG2_E)
    return pl.pallas_call(
        kern,
        grid=(batch, n_heads // hp, nb),
        in_specs=[pl.BlockSpec((BLOCK, w), lambda b, h, c: (b * nb + c, h)),
                  pl.BlockSpec((seq, w), lambda b, h, c: (b, n_heads // hp + h)),
                  pl.BlockSpec((None, w, seq), lambda b, h, c: (b, h, 0)),
                  pl.BlockSpec((hp, 3, BLOCK, BLOCK), lambda b, h, c: (h, 0, 0, 0))],
        out_specs=pl.BlockSpec((BLOCK, w), lambda b, h, c: (b * nb + c, h)),
        out_shape=jax.ShapeDtypeStruct((t, n_heads * HEAD_DIM), BF16),
        scratch_shapes=[pltpu.VMEM((hp, nb, HEAD_DIM), F32),
                        pltpu.VMEM((hp, nb, BLOCK, BLOCK), F32),
                        pltpu.VMEM((hp, seq, BLOCK), BF16)],
        compiler_params=_params(),
        name="moba_attention",
    )(qk, qk, vt, bias)


CONV_TILE = 256
CONV_HALO = 32
CONV_ROWS = 64
LANES = 128
SUBLANES = 8


def _conv_kernel(prev_ref, cur_ref, w_ref, b_ref, g_ref, beta_ref, o_ref, win_ref, y_ref):
    s = pl.program_id(1)
    win_ref[0:CONV_HALO, :] = jnp.where(s > 0, prev_ref[...], 0.0)
    win_ref[CONV_HALO:, :] = cur_ref[...]
    width = cur_ref.shape[1]
    first = CONV_HALO - (CONV_K - 1)
    phases = [[k for k in range(CONV_K) if (first + k) % SUBLANES == a] for a in range(SUBLANES)]
    for lc in range(width // LANES):
        ls = slice(lc * LANES, (lc + 1) * LANES)
        for rc in range(CONV_TILE // CONV_ROWS):
            r0 = rc * CONV_ROWS
            acc = jnp.broadcast_to(b_ref[:, ls], (CONV_ROWS, LANES))
            for a, taps in enumerate(phases):
                rows = CONV_ROWS + (SUBLANES if a else 0)
                z = None
                for k in taps:
                    start = r0 + first + k - a
                    term = win_ref[start:start + rows, ls] * w_ref[k:k + 1, ls]
                    z = term if z is None else z + term
                acc = acc + z[a:a + CONV_ROWS, :]
            y_ref[r0:r0 + CONV_ROWS, ls] = acc
    y = y_ref[...]
    mu = jnp.mean(y, axis=-1, keepdims=True)
    yc = y - mu
    var = jnp.mean(yc * yc, axis=-1, keepdims=True)
    z = yc * lax.rsqrt(var + LN_EPS) * g_ref[...] + beta_ref[...]
    o_ref[...] = (z * jax.nn.sigmoid(z)).astype(o_ref.dtype)


def _conv_module(u, conv_w, conv_b, ln_g, ln_b, layer, batch):
    t, width = u.shape
    seq = t // batch
    tiles = seq // CONV_TILE
    halo_per_tile = CONV_TILE // CONV_HALO
    vec = pl.BlockSpec((None, 1, width), lambda b, s: (layer, 0, 0))
    return pl.pallas_call(
        _conv_kernel,
        grid=(batch, tiles),
        in_specs=[pl.BlockSpec((CONV_HALO, width),
                               lambda b, s: (jnp.maximum((b * tiles + s) * halo_per_tile - 1, 0), 0)),
                  pl.BlockSpec((CONV_TILE, width), lambda b, s: (b * tiles + s, 0)),
                  pl.BlockSpec((None, CONV_K, width), lambda b, s: (layer, 0, 0)),
                  vec, vec, vec],
        out_specs=pl.BlockSpec((CONV_TILE, width), lambda b, s: (b * tiles + s, 0)),
        out_shape=jax.ShapeDtypeStruct((t, width), BF16),
        scratch_shapes=[pltpu.VMEM((CONV_HALO + CONV_TILE, width), F32), pltpu.VMEM((CONV_TILE, width), F32)],
        compiler_params=_params(),
        name="conformer_conv",
    )(u, u, conv_w, conv_b, ln_g, ln_b)


def _layer_norm(y, g, b):
    mu = jnp.mean(y, axis=-1, keepdims=True)
    yc = y - mu
    var = jnp.mean(yc * yc, axis=-1, keepdims=True)
    return yc * lax.rsqrt(var + LN_EPS) * g + b


def _route(logits):
    lane = lax.broadcasted_iota(jnp.int32, logits.shape, 1)
    is_group = lane < N_GROUPS
    gl = jnp.where(is_group, logits, -jnp.inf)
    gmax = jnp.max(gl, axis=-1, keepdims=True)
    g_top = jnp.min(jnp.where(gl == gmax, lane, ROUTER_LANES), axis=-1, keepdims=True)
    p_group = 1.0 / jnp.sum(jnp.where(is_group, jnp.exp(gl - gmax), 0.0), axis=-1, keepdims=True)
    e_idx = lane - N_GROUPS
    in_group = (e_idx >= 0) & (e_idx < N_EXPERTS) & ((e_idx // EXPERTS_PER_GROUP) == g_top)
    el = jnp.where(in_group, logits, -jnp.inf)
    m1 = jnp.max(el, axis=-1, keepdims=True)
    i1 = jnp.min(jnp.where(el == m1, lane, ROUTER_LANES), axis=-1, keepdims=True)
    el2 = jnp.where(lane == i1, -jnp.inf, el)
    m2 = jnp.max(el2, axis=-1, keepdims=True)
    i2 = jnp.min(jnp.where(el2 == m2, lane, ROUTER_LANES), axis=-1, keepdims=True)
    e2 = jnp.exp(m2 - m1)
    w1 = p_group / (1.0 + e2)
    w2 = p_group * e2 / (1.0 + e2)
    return i1, i2, w1, w2


def _outproj_kernel(attn_ref, conv_ref, w_ref, x_ref, g_ref, b_ref, wr_ref, br_ref,
                    h_ref, ri_ref, rw_ref, cnt_ref, carry_ref, wb_ref, *, alpha):
    i = pl.program_id(0)
    _cast_weights_once(i, [(w_ref, wb_ref)])

    @pl.when(i == 0)
    def _():
        carry_ref[...] = jnp.zeros_like(carry_ref)

    ka = attn_ref.shape[1]
    mix = jnp.dot(attn_ref[...], wb_ref[0:ka, :], preferred_element_type=F32)
    mix = mix + jnp.dot(conv_ref[...], wb_ref[ka:, :], preferred_element_type=F32)
    h = _layer_norm(alpha * x_ref[...] + mix, g_ref[...], b_ref[...])
    _store_row_chunks(h_ref, h)
    h_hi = h.astype(BF16)
    h_lo = (h - h_hi.astype(F32)).astype(BF16)
    t_hi = jnp.dot(h_hi, wr_ref[...], preferred_element_type=F32)
    t_lo = jnp.dot(h_lo, wr_ref[:, 0:ROUTER_LANES], preferred_element_type=F32)
    logits = t_hi[:, 0:ROUTER_LANES] + t_hi[:, ROUTER_LANES:] + t_lo + br_ref[...]
    i1, i2, w1, w2 = _route(logits)

    tm = logits.shape[0]
    lane = lax.broadcasted_iota(jnp.int32, logits.shape, 1)
    onehot = jnp.where((lane == i1) | (lane == i2), 1.0, 0.0)
    rr = lax.broadcasted_iota(jnp.int32, (tm, tm), 0)
    cc = lax.broadcasted_iota(jnp.int32, (tm, tm), 1)
    earlier = jnp.where(cc < rr, 1.0, 0.0).astype(BF16)
    before = jnp.dot(earlier, onehot.astype(BF16), preferred_element_type=F32) + carry_ref[...]
    r1 = jnp.sum(jnp.where(lane == i1, before, 0.0), axis=-1, keepdims=True)
    r2 = jnp.sum(jnp.where(lane == i2, before, 0.0), axis=-1, keepdims=True)
    carry_ref[...] += jnp.sum(onehot, axis=0, keepdims=True)
    cnt_ref[...] = jnp.broadcast_to(carry_ref[...], cnt_ref.shape)
    rec = jnp.where(lane == 0, (i1 - N_GROUPS).astype(F32),
                    jnp.where(lane == 1, (i2 - N_GROUPS).astype(F32),
                              jnp.where(lane == 2, r1, jnp.where(lane == 3, r2, 0.0))))
    ri_ref[...] = rec.T[0:ri_ref.shape[0], :].astype(jnp.int32)
    rw_ref[...] = jnp.where(lane == 0, w1, jnp.where(lane == 1, w2, 0.0))


def _outproj(attn, conv, w_out, x, ln_g, ln_b, w_router, b_router, layer, alpha, tm=256):
    t, d = x.shape
    ka, kc = attn.shape[1], conv.shape[1]
    vec = pl.BlockSpec((None, 1, d), lambda i: (layer, 0, 0))
    row = lambda w: pl.BlockSpec((tm, w), lambda i: (i, 0))
    return pl.pallas_call(
        functools.partial(_outproj_kernel, alpha=alpha),
        grid=(t // tm,),
        in_specs=[row(ka), row(kc),
                  pl.BlockSpec((None, ka + kc, d), lambda i: (layer, 0, 0), pipeline_mode=pl.Buffered(1)),
                  row(d), vec, vec,
                  pl.BlockSpec((None, d, 2 * ROUTER_LANES), lambda i: (layer, 0, 0)),
                  pl.BlockSpec((None, 1, ROUTER_LANES), lambda i: (layer, 0, 0))],
        out_specs=[pl.BlockSpec((tm * d // LANES, LANES), lambda i: (i, 0)),
                   pl.BlockSpec((SUBLANES, tm), lambda i: (0, i)), row(ROUTER_LANES),
                   pl.BlockSpec((8, ROUTER_LANES), lambda i: (0, 0))],
        out_shape=[jax.ShapeDtypeStruct((t * d // LANES, LANES), F32),
                   jax.ShapeDtypeStruct((SUBLANES, t), jnp.int32),
                   jax.ShapeDtypeStruct((t, ROUTER_LANES), F32),
                   jax.ShapeDtypeStruct((8, ROUTER_LANES), F32)],
        scratch_shapes=[pltpu.VMEM((1, ROUTER_LANES), F32), pltpu.VMEM((ka + kc, d), BF16)],
        compiler_params=_params(dimension_semantics=("arbitrary",)),
        name="outproj_ln_router",
    )(attn, conv, w_out, x, ln_g, ln_b, w_router, b_router)


MOE_TILE = 256


def _moe_plan(route_i, cnt, n_pairs):
    e = route_i[0:2, :]
    rank = route_i[2:4, :]
    counts = cnt[0, N_GROUPS:N_GROUPS + N_EXPERTS].astype(jnp.int32)
    tiles_e = (counts + MOE_TILE - 1) // MOE_TILE
    tile_end = jnp.cumsum(tiles_e)
    row_start = (tile_end - tiles_e) * MOE_TILE
    start = functools.reduce(jnp.add, [jnp.where(e == k, row_start[k], 0) for k in range(N_EXPERTS)])
    pos = (start + rank).reshape(-1)
    max_tiles = n_pairs // MOE_TILE + N_EXPERTS
    n_used = tile_end[-1:]
    j = jnp.minimum(jnp.arange(max_tiles, dtype=jnp.int32), jnp.maximum(n_used - 1, 0))
    ex = jnp.minimum(jnp.sum((j[:, None] >= tile_end[None, :]).astype(jnp.int32), axis=1), N_EXPERTS - 1)
    after = jnp.minimum(tile_end[ex], jnp.maximum(n_used - 1, 0))
    next_ex = jnp.where(tile_end[ex] < n_used, ex[after], ex)
    parity = jnp.cumsum(jnp.concatenate([jnp.zeros((1,), jnp.int32), (ex[1:] != ex[:-1]).astype(jnp.int32)])) % 2
    pad_lo = jnp.concatenate([row_start + counts, n_used * MOE_TILE])
    pad_hi = jnp.concatenate([tile_end * MOE_TILE, jnp.minimum(n_used + 1, max_tiles) * MOE_TILE])
    return (pos, ex.astype(jnp.int32), next_ex.astype(jnp.int32), parity.astype(jnp.int32),
            n_used.astype(jnp.int32), pad_lo.astype(jnp.int32), pad_hi.astype(jnp.int32))


def _pack_bf16_pair(y):
    n = y.shape[1] // 2
    hi = lax.bitcast_convert_type(y[:, :n].astype(BF16).astype(F32), jnp.uint32)
    lo = lax.bitcast_convert_type(y[:, n:].astype(BF16).astype(F32), jnp.uint32)
    return hi | (lo >> 16)


def _unpack_bf16_pair(u):
    hi = lax.bitcast_convert_type(u & jnp.uint32(0xFFFF0000), F32)
    lo = lax.bitcast_convert_type(u << 16, F32)
    return hi, lo


def _store_row_chunks(ref, val):
    rows, width = val.shape
    n = width // LANES
    for c in range(n):
        ref[pl.ds(c, rows, stride=n), :] = val[:, c * LANES:(c + 1) * LANES]


def _start_alternating(copies):
    for k, cp in enumerate(copies):
        cp.start(priority=k % 2)


def _load_row_chunks(ref, rows, n):
    return jnp.concatenate([ref[pl.ds(c, rows, stride=n), :] for c in range(n)], axis=1)


def _gmm_kernel(pos_ref, ex_ref, next_ref, parity_ref, nused_ref, padlo_ref, padhi_ref, h_ref, wgu_ref, wd_ref,
                ys_ref, inv_ref, xbuf, wgu_f_ref, wd_f_ref, wgu_b_ref, wd_b_ref, sems, wsems, *, layer):
    j = pl.program_id(0)
    n_tiles = pl.num_programs(0)
    chunks = wd_b_ref.shape[1] // LANES
    tm = ys_ref.shape[0] * 2 // chunks
    tokens = pos_ref.shape[0] // 2
    n_used = nused_ref[0]

    def row_copies(tile):
        slot = tile % 2
        return [pltpu.make_async_copy(h_ref.at[pl.ds(inv_ref[tile * tm + r] * chunks, chunks), :],
                                      xbuf.at[slot, pl.ds(r * chunks, chunks), :], sems.at[slot])
                for r in range(tm)]

    def weight_copies(expert, slot):
        return [pltpu.make_async_copy(wgu_ref.at[layer, expert], wgu_f_ref.at[slot], wsems.at[slot]),
                pltpu.make_async_copy(wd_ref.at[layer, expert], wd_f_ref.at[slot], wsems.at[slot])]

    @pl.when(j == 0)
    def _():
        for cp in weight_copies(ex_ref[0], parity_ref[0]):
            cp.start(priority=1)

        def clear(r, carry):
            inv_ref[r] = 0
            return carry
        for k in range(padlo_ref.shape[0]):
            lax.fori_loop(padlo_ref[k], padhi_ref[k], clear, 0)

        def fill(t, carry):
            inv_ref[pos_ref[t]] = t
            inv_ref[pos_ref[tokens + t]] = t
            return carry
        lax.fori_loop(0, tokens, fill, 0, unroll=8)
        for cp in row_copies(0):
            cp.start()

    @pl.when(j < n_used)
    def _():
        @pl.when((j == 0) | (ex_ref[j] != ex_ref[jnp.maximum(j - 1, 0)]))
        def _():
            slot = parity_ref[j]
            for cp in weight_copies(ex_ref[j], slot):
                cp.wait()
            wgu_b_ref[...] = wgu_f_ref[slot].astype(BF16)
            wd_b_ref[...] = wd_f_ref[slot].astype(BF16)

            @pl.when(next_ref[j] != ex_ref[j])
            def _():
                for cp in weight_copies(next_ref[j], 1 - slot):
                    cp.start(priority=1)

        for cp in row_copies(jnp.minimum(j + 1, n_tiles - 1)):
            cp.start()
        for cp in row_copies(j):
            cp.wait()
        f = wd_b_ref.shape[0]
        x = _load_row_chunks(xbuf.at[j % 2], tm, chunks).astype(BF16)
        hg = jnp.dot(x, wgu_b_ref[...], preferred_element_type=F32)
        gate = hg[:, :f]
        act = gate * jax.nn.sigmoid(gate) * hg[:, f:]
        y = jnp.dot(act.astype(BF16), wd_b_ref[...], preferred_element_type=F32)
        _store_row_chunks(ys_ref, _pack_bf16_pair(y))

    @pl.when(j >= n_used)
    def _():
        @pl.when(j == n_used)
        def _():
            for cp in row_copies(j):
                cp.wait()
        ys_ref[...] = jnp.zeros_like(ys_ref)


def _gmm(plan, h_chunks, w_gu, w_down, layer):
    pos, ex, next_ex, parity, n_used, pad_lo, pad_hi = plan
    max_tiles = ex.shape[0]
    d = w_gu.shape[2]
    f2 = w_gu.shape[-1]
    f = w_down.shape[2]
    chunks = d // LANES
    hbm = pl.BlockSpec(memory_space=pl.ANY)
    return pl.pallas_call(
        functools.partial(_gmm_kernel, layer=layer),
        grid_spec=pltpu.PrefetchScalarGridSpec(
            num_scalar_prefetch=7,
            grid=(max_tiles,),
            in_specs=[hbm, hbm, hbm],
            out_specs=pl.BlockSpec((MOE_TILE * chunks // 2, LANES), lambda j, *_: (j, 0)),
            scratch_shapes=[pltpu.SMEM((max_tiles * MOE_TILE,), jnp.int32),
                            pltpu.VMEM((2, MOE_TILE * chunks, LANES), F32),
                            pltpu.VMEM((2, d, f2), F32),
                            pltpu.VMEM((2, f, d), F32),
                            pltpu.VMEM((d, f2), BF16),
                            pltpu.VMEM((f, d), BF16),
                            pltpu.SemaphoreType.DMA((2,)),
                            pltpu.SemaphoreType.DMA((2,))]),
        out_shape=jax.ShapeDtypeStruct((max_tiles * MOE_TILE * chunks // 2, LANES), jnp.uint32),
        compiler_params=_params(dimension_semantics=("arbitrary",)),
        name="moe_grouped_matmul",
    )(pos, ex, next_ex, parity, n_used, pad_lo, pad_hi, h_chunks, w_gu, w_down)


def _ple_kernel(pos_ref, h_ref, rw_ref, ys_ref, g_ref, b_ref, wgate_ref, p_ref, wp_ref, x_ref, xb_ref,
                ybuf, sems, wgate_b_ref, wp_b_ref, *, alpha):
    i = pl.program_id(0)
    n_tiles = pl.num_programs(0)
    tm = rw_ref.shape[0]
    tokens = pos_ref.shape[0] // 2
    yc = ybuf.shape[2] // tm
    hc = h_ref.shape[0] // tm

    def row_copies(tile, slot):
        return [pltpu.make_async_copy(ys_ref.at[pl.ds(pos_ref[s * tokens + tile * tm + r] * yc, yc), :],
                                      ybuf.at[slot, s, pl.ds(r * yc, yc), :], sems.at[slot])
                for r in range(tm) for s in range(2)]

    @pl.when(i == 0)
    def _():
        wgate_b_ref[...] = wgate_ref[...].astype(BF16)
        wp_b_ref[...] = wp_ref[...].astype(BF16)
        _start_alternating(row_copies(0, 0))

    slot = i % 2
    ahead = jnp.minimum(i + 1, n_tiles - 1)
    _start_alternating(row_copies(ahead, 1 - slot))
    for cp in row_copies(i, slot):
        cp.wait()
    rw = rw_ref[...]
    y0_hi, y0_lo = _unpack_bf16_pair(_load_row_chunks(ybuf.at[slot, 0], tm, yc))
    y1_hi, y1_lo = _unpack_bf16_pair(_load_row_chunks(ybuf.at[slot, 1], tm, yc))
    w0, w1 = rw[:, 0:1], rw[:, 1:2]
    f = jnp.concatenate([w0 * y0_hi + w1 * y1_hi, w0 * y0_lo + w1 * y1_lo], axis=1)
    h2 = _layer_norm(alpha * _load_row_chunks(h_ref, tm, hc) + f, g_ref[...], b_ref[...])
    gate = jax.nn.sigmoid(jnp.dot(h2.astype(BF16), wgate_b_ref[...], preferred_element_type=F32))
    pe = jnp.dot(p_ref[...].astype(BF16), wp_b_ref[...], preferred_element_type=F32)
    x_new = h2 + gate * pe
    x_ref[...] = x_new
    xb_ref[...] = x_new.astype(BF16)

    @pl.when(i == n_tiles - 1)
    def _():
        for cp in row_copies(ahead, 1 - slot):
            cp.wait()


def _ple(pos, h_chunks, rw, ys, ln_g, ln_b, w_gate, p, w_ple, layer, alpha, tm=256):
    d = w_gate.shape[-1]
    t = h_chunks.shape[0] * LANES // d
    pd = p.shape[-1]
    vec = pl.BlockSpec((None, 1, d), lambda i, pos: (layer, 0, 0))
    row = lambda w: pl.BlockSpec((tm, w), lambda i, pos: (i, 0))
    return pl.pallas_call(
        functools.partial(_ple_kernel, alpha=alpha),
        grid_spec=pltpu.PrefetchScalarGridSpec(
            num_scalar_prefetch=1,
            grid=(t // tm,),
            in_specs=[pl.BlockSpec((tm * d // LANES, LANES), lambda i, pos: (i, 0)),
                      row(ROUTER_LANES), pl.BlockSpec(memory_space=pl.ANY), vec, vec,
                      pl.BlockSpec((None, d, d), lambda i, pos: (layer, 0, 0), pipeline_mode=pl.Buffered(1)),
                      pl.BlockSpec((None, tm, pd), lambda i, pos: (layer, i, 0)),
                      pl.BlockSpec((None, pd, d), lambda i, pos: (layer, 0, 0), pipeline_mode=pl.Buffered(1))],
            out_specs=[row(d), row(d)],
            scratch_shapes=[pltpu.VMEM((2, 2, tm * d // (2 * LANES), LANES), jnp.uint32),
                            pltpu.SemaphoreType.DMA((2,)),
                            pltpu.VMEM((d, d), BF16), pltpu.VMEM((pd, d), BF16)]),
        out_shape=[jax.ShapeDtypeStruct((t, d), F32), jax.ShapeDtypeStruct((t, d), BF16)],
        compiler_params=_params(dimension_semantics=("arbitrary",)),
        name="gather_ln2_ple",
    )(pos, h_chunks, rw, ys, ln_g, ln_b, w_gate, p, w_ple)


def kernel(x, p, w_in, conv_w, conv_b, conv_ln_g, conv_ln_b, w_out, rel_bias, ln1_g, ln1_b, router_g_w, router_g_b, router_e_w, router_e_b, expert_w_gu, expert_w_down, ln2_g, ln2_b, ple_w, ple_gate_w):
    batch, seq, d = x.shape
    depth = w_in.shape[0]
    n_heads = rel_bias.shape[1]
    attn_w = n_heads * HEAD_DIM
    conv_width = conv_w.shape[2]
    t = batch * seq
    alpha = (2 * depth) ** 0.25

    pad = ROUTER_LANES - N_GROUPS - N_EXPERTS
    w_router = jnp.pad(jnp.concatenate([router_g_w, router_e_w], axis=-1), ((0, 0), (0, 0), (0, pad)))
    w_router_hi = w_router.astype(BF16)
    w_router_lo = (w_router - w_router_hi.astype(F32)).astype(BF16)
    w_router = jnp.concatenate([w_router_hi, w_router_lo], axis=-1)
    b_router = jnp.pad(jnp.concatenate([router_g_b, router_e_b], axis=-1), ((0, 0), (0, pad)))[:, None, :]
    vec3 = lambda v: v[:, None, :]
    p2 = p.reshape(depth, t, p.shape[-1])

    bias = _bias_tiles(rel_bias)
    xf = x.reshape(t, d)
    xb = xf.astype(BF16)
    for i in range(depth):
        qk = _proj_qk(xb, w_in, i, tn=attn_w)
        vt = _proj_vt(xb, w_in, i, batch, attn_w, col_block=2)
        u = _proj_glu(xb, w_in, i, a_col=3 * attn_w, width=conv_width)
        attn = _attention(qk, vt, bias, batch, n_heads)
        conv = _conv_module(u, conv_w, vec3(conv_b), vec3(conv_ln_g), vec3(conv_ln_b), i, batch)
        h, route_i, route_w, cnt = _outproj(attn, conv, w_out, xf, vec3(ln1_g), vec3(ln1_b),
                                            w_router, b_router, i, alpha)
        plan = _moe_plan(route_i, cnt, 2 * t)
        ys = _gmm(plan, h, expert_w_gu, expert_w_down, i)
        xf, xb = _ple(plan[0], h, route_w, ys, vec3(ln2_g), vec3(ln2_b), ple_gate_w, p2, ple_w, i, alpha)
    return xf.reshape(batch, seq, d)
```

```python
import functools
import math

import numpy as np
import jax
import jax.numpy as jnp
from jax import lax
from jax.experimental import pallas as pl
from jax.experimental.pallas import tpu as pltpu

F32 = jnp.float32
BF16 = jnp.bfloat16

HEAD_DIM = 128
BLOCK = 256
TOP_BLOCKS = 3
CONV_K = 31
N_BUCKETS = 32
MAX_DISTANCE = 128
N_GROUPS = 4
EXPERTS_PER_GROUP = 8
N_EXPERTS = N_GROUPS * EXPERTS_PER_GROUP
LN_EPS = 1e-5
LOG2_E = math.log2(math.e)
NEG = -1e30
ROUTER_LANES = 128
VMEM_LIMIT = 56 * 1024 * 1024


def _bucket_thresholds():
    n = np.arange(0, 4 * MAX_DISTANCE)
    max_exact = N_BUCKETS // 2
    nf = np.maximum(n, max_exact).astype(np.float32)
    large = max_exact + (np.log(nf / np.float32(max_exact)) / np.float32(math.log(MAX_DISTANCE / max_exact))
                         * np.float32(N_BUCKETS - max_exact)).astype(np.int32)
    bucket = np.where(n < max_exact, n, np.minimum(large, N_BUCKETS - 1))
    assert np.all(np.diff(bucket) >= 0) and bucket[-1] == N_BUCKETS - 1
    return [int(np.argmax(bucket >= b)) for b in range(N_BUCKETS)]


_BUCKET_START = _bucket_thresholds()


def _params(**kw):
    return pltpu.CompilerParams(vmem_limit_bytes=VMEM_LIMIT, **kw)


def _bias_kernel(rb_ref, out_ref):
    h = pl.program_id(0)
    kj = lax.broadcasted_iota(jnp.int32, (BLOCK, BLOCK), 0)
    qi = lax.broadcasted_iota(jnp.int32, (BLOCK, BLOCK), 1)
    for d in range(3):
        rel = qi - kj + d * BLOCK
        val = jnp.full((BLOCK, BLOCK), rb_ref[0, h], F32)
        for b in range(1, N_BUCKETS):
            val = jnp.where(rel >= _BUCKET_START[b], rb_ref[b, h], val)
        val = val * LOG2_E
        if d == 0:
            val = jnp.where(rel >= 0, val, NEG)
        out_ref[d] = val


def _bias_tiles(rel_bias):
    n_heads = rel_bias.shape[1]
    assert _BUCKET_START[-1] <= BLOCK + 1
    return pl.pallas_call(
        _bias_kernel,
        grid=(n_heads,),
        in_specs=[pl.BlockSpec(memory_space=pltpu.SMEM)],
        out_specs=pl.BlockSpec((None, 3, BLOCK, BLOCK), lambda h: (h, 0, 0, 0)),
        out_shape=jax.ShapeDtypeStruct((n_heads, 3, BLOCK, BLOCK), F32),
        name="t5_bias_tiles",
    )(rel_bias)


def _cast_weights_once(row_tile_id, pairs):
    @pl.when(row_tile_id == 0)
    def _():
        for src, dst in pairs:
            dst[...] = src[...].astype(BF16)


def _mm_kernel(x_ref, w_ref, o_ref, wb_ref):
    _cast_weights_once(pl.program_id(1), [(w_ref, wb_ref)])
    o_ref[...] = jnp.dot(x_ref[...], wb_ref[...], preferred_element_type=F32).astype(o_ref.dtype)


def _proj_qk(xb, w_in, layer, tm=512, tn=1024):
    t, d = xb.shape
    n_out = 2 * tn
    return pl.pallas_call(
        _mm_kernel,
        grid=(n_out // tn, t // tm),
        in_specs=[pl.BlockSpec((tm, d), lambda j, i: (i, 0)),
                  pl.BlockSpec((None, d, tn), lambda j, i: (layer, 0, j))],
        out_specs=pl.BlockSpec((tm, tn), lambda j, i: (i, j)),
        out_shape=jax.ShapeDtypeStruct((t, n_out), BF16),
        scratch_shapes=[pltpu.VMEM((d, tn), BF16)],
        compiler_params=_params(dimension_semantics=("arbitrary", "arbitrary")),
        name="proj_qk",
    )(xb, w_in)


def _proj_vt_kernel(x_ref, w_ref, o_ref, wb_ref):
    _cast_weights_once(pl.program_id(0), [(w_ref, wb_ref)])
    r = jnp.dot(x_ref[...], wb_ref[...], preferred_element_type=F32)
    o_ref[...] = r.T.astype(o_ref.dtype)


def _proj_vt(xb, w_in, layer, batch, width, col_block, tm=512):
    t, d = xb.shape
    seq = t // batch
    per_b = seq // tm
    return pl.pallas_call(
        _proj_vt_kernel,
        grid=(t // tm,),
        in_specs=[pl.BlockSpec((tm, d), lambda i: (i, 0)),
                  pl.BlockSpec((None, d, width), lambda i: (layer, 0, col_block))],
        out_specs=pl.BlockSpec((None, width, tm), lambda i: (i // per_b, 0, i % per_b)),
        out_shape=jax.ShapeDtypeStruct((batch, width, seq), BF16),
        scratch_shapes=[pltpu.VMEM((d, width), BF16)],
        compiler_params=_params(dimension_semantics=("arbitrary",)),
        name="proj_vt",
    )(xb, w_in)


def _proj_glu_kernel(x_ref, wa_ref, wg_ref, o_ref, wab_ref, wgb_ref):
    _cast_weights_once(pl.program_id(1), [(wa_ref, wab_ref), (wg_ref, wgb_ref)])
    x = x_ref[...]
    a = jnp.dot(x, wab_ref[...], preferred_element_type=F32)
    g = jnp.dot(x, wgb_ref[...], preferred_element_type=F32)
    o_ref[...] = a * jax.nn.sigmoid(g)


def _proj_glu(xb, w_in, layer, a_col, width, tm=512, tn=512):
    t, d = xb.shape
    a_blk = a_col // tn
    g_blk = (a_col + width) // tn
    return pl.pallas_call(
        _proj_glu_kernel,
        grid=(width // tn, t // tm),
        in_specs=[pl.BlockSpec((tm, d), lambda j, i: (i, 0)),
                  pl.BlockSpec((None, d, tn), lambda j, i: (layer, 0, a_blk + j)),
                  pl.BlockSpec((None, d, tn), lambda j, i: (layer, 0, g_blk + j))],
        out_specs=pl.BlockSpec((tm, tn), lambda j, i: (i, j)),
        out_shape=jax.ShapeDtypeStruct((t, width), F32),
        scratch_shapes=[pltpu.VMEM((d, tn), BF16), pltpu.VMEM((d, tn), BF16)],
        compiler_params=_params(dimension_semantics=("arbitrary", "arbitrary")),
        name="proj_glu",
    )(xb, w_in, w_in)


ATTN_HEADS_PER_STEP = 4
_NT = (((1,), (1,)), ((), ()))


def _block_penalty(kmean, q, c):
    gate = lax.dot_general(kmean, q.astype(F32), _NT, precision=lax.Precision.HIGHEST,
                           preferred_element_type=F32)
    row = lax.broadcasted_iota(jnp.int32, gate.shape, 0)
    gate_m = jnp.where(row < c, gate, -jnp.inf)
    pens = []
    for n in range(c):
        g_n = gate[n:n + 1, :]
        beats = (gate_m > g_n) | ((gate_m == g_n) & (row < n))
        rank = jnp.sum(beats.astype(F32), axis=0, keepdims=True)
        pens.append(jnp.where(rank < TOP_BLOCKS, 0.0, NEG))
    return pens


def _attn_row(c, hh, q_ref, k_ref, vt_ref, bias_ref, o_ref, kmean_ref, logit_ref, prob_ref, scale):
    hs = slice(hh * HEAD_DIM, (hh + 1) * HEAD_DIM)
    q = q_ref[:, hs]
    keys = (c + 1) * BLOCK
    groups = BLOCK // SUBLANES
    pens = _block_penalty(kmean_ref[hh], q, c) if c > TOP_BLOCKS else None
    m8 = None
    for n in range(c + 1):
        s = lax.dot_general(k_ref[n * BLOCK:(n + 1) * BLOCK, hs], q, _NT, preferred_element_type=F32)
        b = s * scale + bias_ref[hh, min(c - n, 2)]
        if pens is not None and n < c:
            b = b + pens[n]
        logit_ref[hh, n] = b
        bm = jnp.max(b.reshape(groups, SUBLANES, BLOCK), axis=0)
        m8 = bm if m8 is None else jnp.maximum(m8, bm)
    m = jnp.max(m8, axis=0, keepdims=True)
    l8 = jnp.zeros((SUBLANES, BLOCK), F32)
    for n in range(c + 1):
        p = jnp.exp2(logit_ref[hh, n] - m)
        l8 = l8 + jnp.sum(p.reshape(groups, SUBLANES, BLOCK), axis=0)
        prob_ref[hh, n * BLOCK:(n + 1) * BLOCK, :] = p.astype(BF16)
    l = jnp.sum(l8, axis=0, keepdims=True)
    acc = jnp.dot(vt_ref[hs, 0:keys], prob_ref[hh, 0:keys, :], preferred_element_type=F32)
    o_ref[:, hs] = (acc / l).T.astype(o_ref.dtype)


def _attn_kernel(q_ref, k_ref, vt_ref, bias_ref, o_ref, kmean_ref, logit_ref, prob_ref, *, n_blocks, scale):
    c_id = pl.program_id(2)
    heads = q_ref.shape[1] // HEAD_DIM

    @pl.when(c_id == 0)
    def _():
        for hh in range(heads):
            for n in range(n_blocks):
                kb = k_ref[n * BLOCK:(n + 1) * BLOCK, hh * HEAD_DIM:(hh + 1) * HEAD_DIM]
                kmean_ref[hh, n:n + 1, :] = jnp.mean(kb.astype(F32), axis=0, keepdims=True)

    for c in range(n_blocks):
        @pl.when(c_id == c)
        def _(c=c):
            for hh in range(heads):
                _attn_row(c, hh, q_ref, k_ref, vt_ref, bias_ref, o_ref, kmean_ref, logit_ref, prob_ref, scale)


def _attention(qk, vt, bias, batch, n_heads):
    t = qk.shape[0]
    seq = t // batch
    nb = seq // BLOCK
    hp = ATTN_HEADS_PER_STEP
    w = hp * HEAD_DIM
    kern = functools.partial(_attn_kernel, n_blocks=nb, scale=HEAD_DIM ** -0.5 * LOG2_E)
    return pl.pallas_call(
        kern,
        grid=(batch, n_heads // hp, nb),
        in_specs=[pl.BlockSpec((BLOCK, w), lambda b, h, c: (b * nb + c, h)),
                  pl.BlockSpec((seq, w), lambda b, h, c: (b, n_heads // hp + h)),
                  pl.BlockSpec((None, w, seq), lambda b, h, c: (b, h, 0)),
                  pl.BlockSpec((hp, 3, BLOCK, BLOCK), lambda b, h, c: (h, 0, 0, 0))],
        out_specs=pl.BlockSpec((BLOCK, w), lambda b, h, c: (b * nb + c, h)),
        out_shape=jax.ShapeDtypeStruct((t, n_heads * HEAD_DIM), BF16),
        scratch_shapes=[pltpu.VMEM((hp, nb, HEAD_DIM), F32),
                        pltpu.VMEM((hp, nb, BLOCK, BLOCK), F32),
                        pltpu.VMEM((hp, seq, BLOCK), BF16)],
        compiler_params=_params(),
        name="moba_attention",
    )(qk, qk, vt, bias)


CONV_TILE = 256
CONV_HALO = 32
CONV_ROWS = 64
LANES = 128
SUBLANES = 8


def _conv_kernel(prev_ref, cur_ref, w_ref, b_ref, g_ref, beta_ref, o_ref, win_ref, y_ref):
    s = pl.program_id(1)
    win_ref[0:CONV_HALO, :] = jnp.where(s > 0, prev_ref[...], 0.0)
    win_ref[CONV_HALO:, :] = cur_ref[...]
    width = cur_ref.shape[1]
    first = CONV_HALO - (CONV_K - 1)
    phases = [[k for k in range(CONV_K) if (first + k) % SUBLANES == a] for a in range(SUBLANES)]
    for lc in range(width // LANES):
        ls = slice(lc * LANES, (lc + 1) * LANES)
        for rc in range(CONV_TILE // CONV_ROWS):
            r0 = rc * CONV_ROWS
            acc = jnp.broadcast_to(b_ref[:, ls], (CONV_ROWS, LANES))
            for a, taps in enumerate(phases):
                rows = CONV_ROWS + (SUBLANES if a else 0)
                z = None
                for k in taps:
                    start = r0 + first + k - a
                    term = win_ref[start:start + rows, ls] * w_ref[k:k + 1, ls]
                    z = term if z is None else z + term
                acc = acc + z[a:a + CONV_ROWS, :]
            y_ref[r0:r0 + CONV_ROWS, ls] = acc
    y = y_ref[...]
    mu = jnp.mean(y, axis=-1, keepdims=True)
    yc = y - mu
    var = jnp.mean(yc * yc, axis=-1, keepdims=True)
    z = yc * lax.rsqrt(var + LN_EPS) * g_ref[...] + beta_ref[...]
    o_ref[...] = (z * jax.nn.sigmoid(z)).astype(o_ref.dtype)


def _conv_module(u, conv_w, conv_b, ln_g, ln_b, layer, batch):
    t, width = u.shape
    seq = t // batch
    tiles = seq // CONV_TILE
    halo_per_tile = CONV_TILE // CONV_HALO
    vec = pl.BlockSpec((None, 1, width), lambda b, s: (layer, 0, 0))
    return pl.pallas_call(
        _conv_kernel,
        grid=(batch, tiles),
        in_specs=[pl.BlockSpec((CONV_HALO, width),
                               lambda b, s: (jnp.maximum((b * tiles + s) * halo_per_tile - 1, 0), 0)),
                  pl.BlockSpec((CONV_TILE, width), lambda b, s: (b * tiles + s, 0)),
                  pl.BlockSpec((None, CONV_K, width), lambda b, s: (layer, 0, 0)),
                  vec, vec, vec],
        out_specs=pl.BlockSpec((CONV_TILE, width), lambda b, s: (b * tiles + s, 0)),
        out_shape=jax.ShapeDtypeStruct((t, width), BF16),
        scratch_shapes=[pltpu.VMEM((CONV_HALO + CONV_TILE, width), F32), pltpu.VMEM((CONV_TILE, width), F32)],
        compiler_params=_params(),
        name="conformer_conv",
    )(u, u, conv_w, conv_b, ln_g, ln_b)


def _layer_norm(y, g, b):
    mu = jnp.mean(y, axis=-1, keepdims=True)
    yc = y - mu
    var = jnp.mean(yc * yc, axis=-1, keepdims=True)
    return yc * lax.rsqrt(var + LN_EPS) * g + b


def _route(logits):
    lane = lax.broadcasted_iota(jnp.int32, logits.shape, 1)
    is_group = lane < N_GROUPS
    gl = jnp.where(is_group, logits, -jnp.inf)
    gmax = jnp.max(gl, axis=-1, keepdims=True)
    g_top = jnp.min(jnp.where(gl == gmax, lane, ROUTER_LANES), axis=-1, keepdims=True)
    p_group = 1.0 / jnp.sum(jnp.where(is_group, jnp.exp(gl - gmax), 0.0), axis=-1, keepdims=True)
    e_idx = lane - N_GROUPS
    in_group = (e_idx >= 0) & (e_idx < N_EXPERTS) & ((e_idx // EXPERTS_PER_GROUP) == g_top)
    el = jnp.where(in_group, logits, -jnp.inf)
    m1 = jnp.max(el, axis=-1, keepdims=True)
    i1 = jnp.min(jnp.where(el == m1, lane, ROUTER_LANES), axis=-1, keepdims=True)
    el2 = jnp.where(lane == i1, -jnp.inf, el)
    m2 = jnp.max(el2, axis=-1, keepdims=True)
    i2 = jnp.min(jnp.where(el2 == m2, lane, ROUTER_LANES), axis=-1, keepdims=True)
    e2 = jnp.exp(m2 - m1)
    w1 = p_group / (1.0 + e2)
    w2 = p_group * e2 / (1.0 + e2)
    return i1, i2, w1, w2


def _outproj_kernel(attn_ref, conv_ref, w_ref, x_ref, g_ref, b_ref, wr_ref, br_ref,
                    h_ref, ri_ref, rw_ref, cnt_ref, carry_ref, wb_ref, *, alpha):
    i = pl.program_id(0)
    _cast_weights_once(i, [(w_ref, wb_ref)])

    @pl.when(i == 0)
    def _():
        carry_ref[...] = jnp.zeros_like(carry_ref)

    ka = attn_ref.shape[1]
    mix = jnp.dot(attn_ref[...], wb_ref[0:ka, :], preferred_element_type=F32)
    mix = mix + jnp.dot(conv_ref[...], wb_ref[ka:, :], preferred_element_type=F32)
    h = _layer_norm(alpha * x_ref[...] + mix, g_ref[...], b_ref[...])
    _store_row_chunks(h_ref, h)
    h_hi = h.astype(BF16)
    h_lo = (h - h_hi.astype(F32)).astype(BF16)
    t_hi = jnp.dot(h_hi, wr_ref[...], preferred_element_type=F32)
    t_lo = jnp.dot(h_lo, wr_ref[:, 0:ROUTER_LANES], preferred_element_type=F32)
    logits = t_hi[:, 0:ROUTER_LANES] + t_hi[:, ROUTER_LANES:] + t_lo + br_ref[...]
    i1, i2, w1, w2 = _route(logits)

    tm = logits.shape[0]
    lane = lax.broadcasted_iota(jnp.int32, logits.shape, 1)
    onehot = jnp.where((lane == i1) | (lane == i2), 1.0, 0.0)
    rr = lax.broadcasted_iota(jnp.int32, (tm, tm), 0)
    cc = lax.broadcasted_iota(jnp.int32, (tm, tm), 1)
    earlier = jnp.where(cc < rr, 1.0, 0.0).astype(BF16)
    before = jnp.dot(earlier, onehot.astype(BF16), preferred_element_type=F32) + carry_ref[...]
    r1 = jnp.sum(jnp.where(lane == i1, before, 0.0), axis=-1, keepdims=True)
    r2 = jnp.sum(jnp.where(lane == i2, before, 0.0), axis=-1, keepdims=True)
    carry_ref[...] += jnp.sum(onehot, axis=0, keepdims=True)
    cnt_ref[...] = jnp.broadcast_to(carry_ref[...], cnt_ref.shape)
    rec = jnp.where(lane == 0, (i1 - N_GROUPS).astype(F32),
                    jnp.where(lane == 1, (i2 - N_GROUPS).astype(F32),
                              jnp.where(lane == 2, r1, jnp.where(lane == 3, r2, 0.0))))
    ri_ref[...] = rec.T[0:ri_ref.shape[0], :].astype(jnp.int32)
    rw_ref[...] = jnp.where(lane == 0, w1, jnp.where(lane == 1, w2, 0.0))


def _outproj(attn, conv, w_out, x, ln_g, ln_b, w_router, b_router, layer, alpha, tm=256):
    t, d = x.shape
    ka, kc = attn.shape[1], conv.shape[1]
    vec = pl.BlockSpec((None, 1, d), lambda i: (layer, 0, 0))
    row = lambda w: pl.BlockSpec((tm, w), lambda i: (i, 0))
    return pl.pallas_call(
        functools.partial(_outproj_kernel, alpha=alpha),
        grid=(t // tm,),
        in_specs=[row(ka), row(kc),
                  pl.BlockSpec((None, ka + kc, d), lambda i: (layer, 0, 0), pipeline_mode=pl.Buffered(1)),
                  row(d), vec, vec,
                  pl.BlockSpec((None, d, 2 * ROUTER_LANES), lambda i: (layer, 0, 0)),
                  pl.BlockSpec((None, 1, ROUTER_LANES), lambda i: (layer, 0, 0))],
        out_specs=[pl.BlockSpec((tm * d // LANES, LANES), lambda i: (i, 0)),
                   pl.BlockSpec((SUBLANES, tm), lambda i: (0, i)), row(ROUTER_LANES),
                   pl.BlockSpec((8, ROUTER_LANES), lambda i: (0, 0))],
        out_shape=[jax.ShapeDtypeStruct((t * d // LANES, LANES), F32),
                   jax.ShapeDtypeStruct((SUBLANES, t), jnp.int32),
                   jax.ShapeDtypeStruct((t, ROUTER_LANES), F32),
                   jax.ShapeDtypeStruct((8, ROUTER_LANES), F32)],
        scratch_shapes=[pltpu.VMEM((1, ROUTER_LANES), F32), pltpu.VMEM((ka + kc, d), BF16)],
        compiler_params=_params(dimension_semantics=("arbitrary",)),
        name="outproj_ln_router",
    )(attn, conv, w_out, x, ln_g, ln_b, w_router, b_router)


MOE_TILE = 256


def _moe_plan(route_i, cnt, n_rows):
    e = route_i[0:2, :]
    rank = route_i[2:4, :]
    counts = cnt[0, N_GROUPS:N_GROUPS + N_EXPERTS].astype(jnp.int32)
    ends = jnp.cumsum(counts)
    offs = ends - counts
    start = functools.reduce(jnp.add, [jnp.where(e == k, offs[k], 0) for k in range(N_EXPERTS)])
    pos = (start + rank).reshape(-1)
    n_tiles = n_rows // MOE_TILE
    first_tile = offs // MOE_TILE
    n_items_e = jnp.where(counts > 0, (ends - 1) // MOE_TILE - first_tile + 1, 0)
    item_end = jnp.cumsum(n_items_e)
    item_start = item_end - n_items_e
    n_items = n_tiles + N_EXPERTS - 1
    j = jnp.arange(n_items, dtype=jnp.int32)
    valid = j < item_end[-1]
    jj = jnp.clip(j, 0, jnp.maximum(item_end[-1] - 1, 0))
    ex = jnp.minimum(jnp.sum((jj[:, None] >= item_end[None, :]).astype(jnp.int32), axis=1), N_EXPERTS - 1)
    tile = first_tile[ex] + (jj - item_start[ex])
    lo = jnp.where(valid, jnp.maximum(offs[ex], tile * MOE_TILE) - tile * MOE_TILE, 0)
    hi = jnp.where(valid, jnp.minimum(ends[ex], (tile + 1) * MOE_TILE) - tile * MOE_TILE, 0)
    return pos, tile.astype(jnp.int32), ex.astype(jnp.int32), lo.astype(jnp.int32), hi.astype(jnp.int32)


def _pack_bf16_pair(y):
    n = y.shape[1] // 2
    hi = lax.bitcast_convert_type(y[:, :n].astype(BF16).astype(F32), jnp.uint32)
    lo = lax.bitcast_convert_type(y[:, n:].astype(BF16).astype(F32), jnp.uint32)
    return hi | (lo >> 16)


def _unpack_bf16_pair(u):
    hi = lax.bitcast_convert_type(u & jnp.uint32(0xFFFF0000), F32)
    lo = lax.bitcast_convert_type(u << 16, F32)
    return hi, lo


def _store_row_chunks(ref, val):
    rows, width = val.shape
    n = width // LANES
    for c in range(n):
        ref[pl.ds(c, rows, stride=n), :] = val[:, c * LANES:(c + 1) * LANES]


def _start_alternating(copies):
    for k, cp in enumerate(copies):
        cp.start(priority=k % 2)


def _load_row_chunks(ref, rows, n):
    return jnp.concatenate([ref[pl.ds(c, rows, stride=n), :] for c in range(n)], axis=1)


def _gmm_kernel(pos_ref, tile_ref, exp_ref, lo_ref, hi_ref, h_ref, wgu_ref, wd_ref, ys_ref,
                inv_ref, xbuf, acc_ref, wgu_b_ref, wd_b_ref, sems):
    j = pl.program_id(0)
    n_items = pl.num_programs(0)
    tm = acc_ref.shape[0]
    chunks = acc_ref.shape[1] // LANES
    tokens = pos_ref.shape[0] // 2
    cur = tile_ref[j]
    nxt = tile_ref[jnp.minimum(j + 1, n_items - 1)]
    first = (j == 0) | (cur != tile_ref[jnp.maximum(j - 1, 0)])
    last = (j == n_items - 1) | (nxt != cur)

    def row_copies(tile):
        slot = tile % 2
        return [pltpu.make_async_copy(h_ref.at[pl.ds(inv_ref[tile * tm + r] * chunks, chunks), :],
                                      xbuf.at[slot, pl.ds(r * chunks, chunks), :], sems.at[slot])
                for r in range(tm)]

    @pl.when(j == 0)
    def _():
        def fill(t, carry):
            inv_ref[pos_ref[t]] = t
            inv_ref[pos_ref[tokens + t]] = t
            return carry
        lax.fori_loop(0, tokens, fill, 0, unroll=8)
        for cp in row_copies(cur):
            cp.start()

    @pl.when((j < n_items - 1) & (nxt != cur))
    def _():
        for cp in row_copies(nxt):
            cp.start()

    @pl.when(first)
    def _():
        for cp in row_copies(cur):
            cp.wait()

    @pl.when((j == 0) | (exp_ref[j] != exp_ref[jnp.maximum(j - 1, 0)]))
    def _():
        wgu_b_ref[...] = wgu_ref[...].astype(BF16)
        wd_b_ref[...] = wd_ref[...].astype(BF16)

    f = wd_ref.shape[0]
    x = _load_row_chunks(xbuf.at[cur % 2], tm, chunks).astype(BF16)
    hg = jnp.dot(x, wgu_b_ref[...], preferred_element_type=F32)
    gate = hg[:, :f]
    row = lax.broadcasted_iota(jnp.int32, gate.shape, 0)
    mine = (row >= lo_ref[j]) & (row < hi_ref[j])
    act = jnp.where(mine, gate * jax.nn.sigmoid(gate) * hg[:, f:], 0.0)
    y = jnp.dot(act.astype(BF16), wd_b_ref[...], preferred_element_type=F32)

    @pl.when(first)
    def _():
        acc_ref[...] = y

    @pl.when(jnp.logical_not(first))
    def _():
        acc_ref[...] += y

    @pl.when(last)
    def _():
        _store_row_chunks(ys_ref, _pack_bf16_pair(acc_ref[...]))


def _gmm(plan, h_chunks, w_gu, w_down, layer):
    pos, tile, ex, lo, hi = plan
    n_rows = pos.shape[0]
    d = w_gu.shape[2]
    f2 = w_gu.shape[-1]
    f = w_down.shape[2]
    chunks = d // LANES
    return pl.pallas_call(
        _gmm_kernel,
        grid_spec=pltpu.PrefetchScalarGridSpec(
            num_scalar_prefetch=5,
            grid=(tile.shape[0],),
            in_specs=[pl.BlockSpec(memory_space=pl.ANY),
                      pl.BlockSpec((None, None, d, f2), lambda j, pos, tile, ex, lo, hi: (layer, ex[j], 0, 0)),
                      pl.BlockSpec((None, None, f, d), lambda j, pos, tile, ex, lo, hi: (layer, ex[j], 0, 0))],
            out_specs=pl.BlockSpec((MOE_TILE * chunks // 2, LANES),
                                   lambda j, pos, tile, ex, lo, hi: (tile[j], 0)),
            scratch_shapes=[pltpu.SMEM((n_rows,), jnp.int32),
                            pltpu.VMEM((2, MOE_TILE * chunks, LANES), F32),
                            pltpu.VMEM((MOE_TILE, d), F32),
                            pltpu.VMEM((d, f2), BF16),
                            pltpu.VMEM((f, d), BF16),
                            pltpu.SemaphoreType.DMA((2,))]),
        out_shape=jax.ShapeDtypeStruct((n_rows * chunks // 2, LANES), jnp.uint32),
        compiler_params=_params(dimension_semantics=("arbitrary",)),
        name="moe_grouped_matmul",
    )(pos, tile, ex, lo, hi, h_chunks, w_gu, w_down)


def _ple_kernel(pos_ref, h_ref, rw_ref, ys_ref, g_ref, b_ref, wgate_ref, p_ref, wp_ref, x_ref, xb_ref,
                ybuf, sems, wgate_b_ref, wp_b_ref, *, alpha):
    i = pl.program_id(0)
    n_tiles = pl.num_programs(0)
    tm = rw_ref.shape[0]
    tokens = pos_ref.shape[0] // 2
    yc = ybuf.shape[2] // tm
    hc = h_ref.shape[0] // tm

    def row_copies(tile, slot):
        return [pltpu.make_async_copy(ys_ref.at[pl.ds(pos_ref[s * tokens + tile * tm + r] * yc, yc), :],
                                      ybuf.at[slot, s, pl.ds(r * yc, yc), :], sems.at[slot])
                for r in range(tm) for s in range(2)]

    @pl.when(i == 0)
    def _():
        wgate_b_ref[...] = wgate_ref[...].astype(BF16)
        wp_b_ref[...] = wp_ref[...].astype(BF16)
        _start_alternating(row_copies(0, 0))

    slot = i % 2
    ahead = jnp.minimum(i + 1, n_tiles - 1)
    _start_alternating(row_copies(ahead, 1 - slot))
    for cp in row_copies(i, slot):
        cp.wait()
    rw = rw_ref[...]
    y0_hi, y0_lo = _unpack_bf16_pair(_load_row_chunks(ybuf.at[slot, 0], tm, yc))
    y1_hi, y1_lo = _unpack_bf16_pair(_load_row_chunks(ybuf.at[slot, 1], tm, yc))
    w0, w1 = rw[:, 0:1], rw[:, 1:2]
    f = jnp.concatenate([w0 * y0_hi + w1 * y1_hi, w0 * y0_lo + w1 * y1_lo], axis=1)
    h2 = _layer_norm(alpha * _load_row_chunks(h_ref, tm, hc) + f, g_ref[...], b_ref[...])
    gate = jax.nn.sigmoid(jnp.dot(h2.astype(BF16), wgate_b_ref[...], preferred_element_type=F32))
    pe = jnp.dot(p_ref[...].astype(BF16), wp_b_ref[...], preferred_element_type=F32)
    x_new = h2 + gate * pe
    x_ref[...] = x_new
    xb_ref[...] = x_new.astype(BF16)

    @pl.when(i == n_tiles - 1)
    def _():
        for cp in row_copies(ahead, 1 - slot):
            cp.wait()


def _ple(pos, h_chunks, rw, ys, ln_g, ln_b, w_gate, p, w_ple, layer, alpha, tm=256):
    d = w_gate.shape[-1]
    t = h_chunks.shape[0] * LANES // d
    pd = p.shape[-1]
    vec = pl.BlockSpec((None, 1, d), lambda i, pos: (layer, 0, 0))
    row = lambda w: pl.BlockSpec((tm, w), lambda i, pos: (i, 0))
    return pl.pallas_call(
        functools.partial(_ple_kernel, alpha=alpha),
        grid_spec=pltpu.PrefetchScalarGridSpec(
            num_scalar_prefetch=1,
            grid=(t // tm,),
            in_specs=[pl.BlockSpec((tm * d // LANES, LANES), lambda i, pos: (i, 0)),
                      row(ROUTER_LANES), pl.BlockSpec(memory_space=pl.ANY), vec, vec,
                      pl.BlockSpec((None, d, d), lambda i, pos: (layer, 0, 0), pipeline_mode=pl.Buffered(1)),
                      pl.BlockSpec((None, tm, pd), lambda i, pos: (layer, i, 0)),
                      pl.BlockSpec((None, pd, d), lambda i, pos: (layer, 0, 0), pipeline_mode=pl.Buffered(1))],
            out_specs=[row(d), row(d)],
            scratch_shapes=[pltpu.VMEM((2, 2, tm * d // (2 * LANES), LANES), jnp.uint32),
                            pltpu.SemaphoreType.DMA((2,)),
                            pltpu.VMEM((d, d), BF16), pltpu.VMEM((pd, d), BF16)]),
        out_shape=[jax.ShapeDtypeStruct((t, d), F32), jax.ShapeDtypeStruct((t, d), BF16)],
        compiler_params=_params(dimension_semantics=("arbitrary",)),
        name="gather_ln2_ple",
    )(pos, h_chunks, rw, ys, ln_g, ln_b, w_gate, p, w_ple)


def kernel(x, p, w_in, conv_w, conv_b, conv_ln_g, conv_ln_b, w_out, rel_bias, ln1_g, ln1_b, router_g_w, router_g_b, router_e_w, router_e_b, expert_w_gu, expert_w_down, ln2_g, ln2_b, ple_w, ple_gate_w):
    batch, seq, d = x.shape
    depth = w_in.shape[0]
    n_heads = rel_bias.shape[1]
    attn_w = n_heads * HEAD_DIM
    conv_width = conv_w.shape[2]
    t = batch * seq
    alpha = (2 * depth) ** 0.25

    pad = ROUTER_LANES - N_GROUPS - N_EXPERTS
    w_router = jnp.pad(jnp.concatenate([router_g_w, router_e_w], axis=-1), ((0, 0), (0, 0), (0, pad)))
    w_router_hi = w_router.astype(BF16)
    w_router_lo = (w_router - w_router_hi.astype(F32)).astype(BF16)
    w_router = jnp.concatenate([w_router_hi, w_router_lo], axis=-1)
    b_router = jnp.pad(jnp.concatenate([router_g_b, router_e_b], axis=-1), ((0, 0), (0, pad)))[:, None, :]
    vec3 = lambda v: v[:, None, :]
    p2 = p.reshape(depth, t, p.shape[-1])

    bias = _bias_tiles(rel_bias)
    xf = x.reshape(t, d)
    xb = xf.astype(BF16)
    for i in range(depth):
        qk = _proj_qk(xb, w_in, i, tn=attn_w)
        vt = _proj_vt(xb, w_in, i, batch, attn_w, col_block=2)
        u = _proj_glu(xb, w_in, i, a_col=3 * attn_w, width=conv_width)
        attn = _attention(qk, vt, bias, batch, n_heads)
        conv = _conv_module(u, conv_w, vec3(conv_b), vec3(conv_ln_g), vec3(conv_ln_b), i, batch)
        h, route_i, route_w, cnt = _outproj(attn, conv, w_out, xf, vec3(ln1_g), vec3(ln1_b),
                                            w_router, b_router, i, alpha)
        plan = _moe_plan(route_i, cnt, 2 * t)
        ys = _gmm(plan, h, expert_w_gu, expert_w_down, i)
        xf, xb = _ple(plan[0], h, route_w, ys, vec3(ln2_g), vec3(ln2_b), ple_gate_w, p2, ple_w, i, alpha)
    return xf.reshape(batch, seq, d)
```

```python
import functools
import math

import numpy as np
import jax
import jax.numpy as jnp
from jax import lax
from jax.experimental import pallas as pl
from jax.experimental.pallas import tpu as pltpu

F32 = jnp.float32
BF16 = jnp.bfloat16

HEAD_DIM = 128
BLOCK = 256
TOP_BLOCKS = 3
CONV_K = 31
N_BUCKETS = 32
MAX_DISTANCE = 128
N_GROUPS = 4
EXPERTS_PER_GROUP = 8
N_EXPERTS = N_GROUPS * EXPERTS_PER_GROUP
LN_EPS = 1e-5
LOG2_E = math.log2(math.e)
NEG = -1e30
ROUTER_LANES = 128
VMEM_LIMIT = 56 * 1024 * 1024


def _bucket_thresholds():
    n = np.arange(0, 4 * MAX_DISTANCE)
    max_exact = N_BUCKETS // 2
    nf = np.maximum(n, max_exact).astype(np.float32)
    large = max_exact + (np.log(nf / np.float32(max_exact)) / np.float32(math.log(MAX_DISTANCE / max_exact))
                         * np.float32(N_BUCKETS - max_exact)).astype(np.int32)
    bucket = np.where(n < max_exact, n, np.minimum(large, N_BUCKETS - 1))
    assert np.all(np.diff(bucket) >= 0) and bucket[-1] == N_BUCKETS - 1
    return [int(np.argmax(bucket >= b)) for b in range(N_BUCKETS)]


_BUCKET_START = _bucket_thresholds()


def _params(**kw):
    return pltpu.CompilerParams(vmem_limit_bytes=VMEM_LIMIT, **kw)


def _bias_kernel(rb_ref, out_ref):
    h = pl.program_id(0)
    kj = lax.broadcasted_iota(jnp.int32, (BLOCK, BLOCK), 0)
    qi = lax.broadcasted_iota(jnp.int32, (BLOCK, BLOCK), 1)
    for d in range(3):
        rel = qi - kj + d * BLOCK
        val = jnp.full((BLOCK, BLOCK), rb_ref[0, h], F32)
        for b in range(1, N_BUCKETS):
            val = jnp.where(rel >= _BUCKET_START[b], rb_ref[b, h], val)
        val = val * LOG2_E
        if d == 0:
            val = jnp.where(rel >= 0, val, NEG)
        out_ref[d] = val


def _bias_tiles(rel_bias):
    n_heads = rel_bias.shape[1]
    assert _BUCKET_START[-1] <= BLOCK + 1
    return pl.pallas_call(
        _bias_kernel,
        grid=(n_heads,),
        in_specs=[pl.BlockSpec(memory_space=pltpu.SMEM)],
        out_specs=pl.BlockSpec((None, 3, BLOCK, BLOCK), lambda h: (h, 0, 0, 0)),
        out_shape=jax.ShapeDtypeStruct((n_heads, 3, BLOCK, BLOCK), F32),
        name="t5_bias_tiles",
    )(rel_bias)


def _cast_weights_once(row_tile_id, pairs):
    @pl.when(row_tile_id == 0)
    def _():
        for src, dst in pairs:
            dst[...] = src[...].astype(BF16)


def _mm_kernel(x_ref, w_ref, o_ref, wb_ref):
    _cast_weights_once(pl.program_id(1), [(w_ref, wb_ref)])
    o_ref[...] = jnp.dot(x_ref[...], wb_ref[...], preferred_element_type=F32).astype(o_ref.dtype)


def _proj_qk(xb, w_in, layer, tm=1024, tn=1024):
    t, d = xb.shape
    n_out = 2 * tn
    return pl.pallas_call(
        _mm_kernel,
        grid=(n_out // tn, t // tm),
        in_specs=[pl.BlockSpec((tm, d), lambda j, i: (i, 0)),
                  pl.BlockSpec((None, d, tn), lambda j, i: (layer, 0, j))],
        out_specs=pl.BlockSpec((tm, tn), lambda j, i: (i, j)),
        out_shape=jax.ShapeDtypeStruct((t, n_out), BF16),
        scratch_shapes=[pltpu.VMEM((d, tn), BF16)],
        compiler_params=_params(dimension_semantics=("arbitrary", "arbitrary")),
        name="proj_qk",
    )(xb, w_in)


def _proj_vt_kernel(x_ref, w_ref, o_ref, wb_ref):
    _cast_weights_once(pl.program_id(0), [(w_ref, wb_ref)])
    r = jnp.dot(x_ref[...], wb_ref[...], preferred_element_type=F32)
    o_ref[...] = r.T.astype(o_ref.dtype)


def _proj_vt(xb, w_in, layer, batch, width, col_block, tm=512):
    t, d = xb.shape
    seq = t // batch
    per_b = seq // tm
    return pl.pallas_call(
        _proj_vt_kernel,
        grid=(t // tm,),
        in_specs=[pl.BlockSpec((tm, d), lambda i: (i, 0)),
                  pl.BlockSpec((None, d, width), lambda i: (layer, 0, col_block))],
        out_specs=pl.BlockSpec((None, width, tm), lambda i: (i // per_b, 0, i % per_b)),
        out_shape=jax.ShapeDtypeStruct((batch, width, seq), BF16),
        scratch_shapes=[pltpu.VMEM((d, width), BF16)],
        compiler_params=_params(dimension_semantics=("arbitrary",)),
        name="proj_vt",
    )(xb, w_in)


def _proj_glu_kernel(x_ref, wa_ref, wg_ref, o_ref, wab_ref, wgb_ref):
    _cast_weights_once(pl.program_id(1), [(wa_ref, wab_ref), (wg_ref, wgb_ref)])
    x = x_ref[...]
    a = jnp.dot(x, wab_ref[...], preferred_element_type=F32)
    g = jnp.dot(x, wgb_ref[...], preferred_element_type=F32)
    o_ref[...] = a * jax.nn.sigmoid(g)


def _proj_glu(xb, w_in, layer, a_col, width, tm=1024, tn=512):
    t, d = xb.shape
    a_blk = a_col // tn
    g_blk = (a_col + width) // tn
    return pl.pallas_call(
        _proj_glu_kernel,
        grid=(width // tn, t // tm),
        in_specs=[pl.BlockSpec((tm, d), lambda j, i: (i, 0)),
                  pl.BlockSpec((None, d, tn), lambda j, i: (layer, 0, a_blk + j)),
                  pl.BlockSpec((None, d, tn), lambda j, i: (layer, 0, g_blk + j))],
        out_specs=pl.BlockSpec((tm, tn), lambda j, i: (i, j)),
        out_shape=jax.ShapeDtypeStruct((t, width), F32),
        scratch_shapes=[pltpu.VMEM((d, tn), BF16), pltpu.VMEM((d, tn), BF16)],
        compiler_params=_params(dimension_semantics=("arbitrary", "arbitrary")),
        name="proj_glu",
    )(xb, w_in, w_in)


ATTN_HEADS_PER_STEP = 4
_NT = (((1,), (1,)), ((), ()))


def _block_penalty(kmean, q, c):
    gate = lax.dot_general(kmean, q.astype(F32), _NT, precision=lax.Precision.HIGHEST,
                           preferred_element_type=F32)
    row = lax.broadcasted_iota(jnp.int32, gate.shape, 0)
    gate_m = jnp.where(row < c, gate, -jnp.inf)
    pens = []
    for n in range(c):
        g_n = gate[n:n + 1, :]
        beats = (gate_m > g_n) | ((gate_m == g_n) & (row < n))
        rank = jnp.sum(beats.astype(F32), axis=0, keepdims=True)
        pens.append(jnp.where(rank < TOP_BLOCKS, 0.0, NEG))
    return pens


def _attn_row(c, hh, q_ref, k_ref, vt_ref, bias_ref, o_ref, kmean_ref, logit_ref, prob_ref, scale):
    hs = slice(hh * HEAD_DIM, (hh + 1) * HEAD_DIM)
    q = q_ref[:, hs]
    keys = (c + 1) * BLOCK
    groups = BLOCK // SUBLANES
    pens = _block_penalty(kmean_ref[hh], q, c) if c > TOP_BLOCKS else None
    m8 = None
    for n in range(c + 1):
        s = lax.dot_general(k_ref[n * BLOCK:(n + 1) * BLOCK, hs], q, _NT, preferred_element_type=F32)
        b = s * scale + bias_ref[hh, min(c - n, 2)]
        if pens is not None and n < c:
            b = b + pens[n]
        logit_ref[hh, n] = b
        bm = jnp.max(b.reshape(groups, SUBLANES, BLOCK), axis=0)
        m8 = bm if m8 is None else jnp.maximum(m8, bm)
    m = jnp.max(m8, axis=0, keepdims=True)
    l8 = jnp.zeros((SUBLANES, BLOCK), F32)
    for n in range(c + 1):
        p = jnp.exp2(logit_ref[hh, n] - m)
        l8 = l8 + jnp.sum(p.reshape(groups, SUBLANES, BLOCK), axis=0)
        prob_ref[hh, n * BLOCK:(n + 1) * BLOCK, :] = p.astype(BF16)
    l = jnp.sum(l8, axis=0, keepdims=True)
    acc = jnp.dot(vt_ref[hs, 0:keys], prob_ref[hh, 0:keys, :], preferred_element_type=F32)
    o_ref[:, hs] = (acc / l).T.astype(o_ref.dtype)


def _attn_kernel(q_ref, k_ref, vt_ref, bias_ref, o_ref, kmean_ref, logit_ref, prob_ref, *, n_blocks, scale):
    c_id = pl.program_id(2)
    heads = q_ref.shape[1] // HEAD_DIM

    @pl.when(c_id == 0)
    def _():
        for hh in range(heads):
            for n in range(n_blocks):
                kb = k_ref[n * BLOCK:(n + 1) * BLOCK, hh * HEAD_DIM:(hh + 1) * HEAD_DIM]
                kmean_ref[hh, n:n + 1, :] = jnp.mean(kb.astype(F32), axis=0, keepdims=True)

    for c in range(n_blocks):
        @pl.when(c_id == c)
        def _(c=c):
            for hh in range(heads):
                _attn_row(c, hh, q_ref, k_ref, vt_ref, bias_ref, o_ref, kmean_ref, logit_ref, prob_ref, scale)


def _attention(qk, vt, bias, batch, n_heads):
    t = qk.shape[0]
    seq = t // batch
    nb = seq // BLOCK
    hp = ATTN_HEADS_PER_STEP
    w = hp * HEAD_DIM
    kern = functools.partial(_attn_kernel, n_blocks=nb, scale=HEAD_DIM ** -0.5 * LOG2_E)
    return pl.pallas_call(
        kern,
        grid=(batch, n_heads // hp, nb),
        in_specs=[pl.BlockSpec((BLOCK, w), lambda b, h, c: (b * nb + c, h)),
                  pl.BlockSpec((seq, w), lambda b, h, c: (b, n_heads // hp + h)),
                  pl.BlockSpec((None, w, seq), lambda b, h, c: (b, h, 0)),
                  pl.BlockSpec((hp, 3, BLOCK, BLOCK), lambda b, h, c: (h, 0, 0, 0))],
        out_specs=pl.BlockSpec((BLOCK, w), lambda b, h, c: (b * nb + c, h)),
        out_shape=jax.ShapeDtypeStruct((t, n_heads * HEAD_DIM), BF16),
        scratch_shapes=[pltpu.VMEM((hp, nb, HEAD_DIM), F32),
                        pltpu.VMEM((hp, nb, BLOCK, BLOCK), F32),
                        pltpu.VMEM((hp, seq, BLOCK), BF16)],
        compiler_params=_params(),
        name="moba_attention",
    )(qk, qk, vt, bias)


CONV_TILE = 256
CONV_HALO = 32
CONV_ROWS = 64
LANES = 128
SUBLANES = 8


def _conv_kernel(prev_ref, cur_ref, w_ref, b_ref, g_ref, beta_ref, o_ref, win_ref, y_ref):
    s = pl.program_id(1)
    win_ref[0:CONV_HALO, :] = jnp.where(s > 0, prev_ref[...], 0.0)
    win_ref[CONV_HALO:, :] = cur_ref[...]
    width = cur_ref.shape[1]
    first = CONV_HALO - (CONV_K - 1)
    phases = [[k for k in range(CONV_K) if (first + k) % SUBLANES == a] for a in range(SUBLANES)]
    for lc in range(width // LANES):
        ls = slice(lc * LANES, (lc + 1) * LANES)
        for rc in range(CONV_TILE // CONV_ROWS):
            r0 = rc * CONV_ROWS
            acc = jnp.broadcast_to(b_ref[:, ls], (CONV_ROWS, LANES))
            for a, taps in enumerate(phases):
                rows = CONV_ROWS + (SUBLANES if a else 0)
                z = None
                for k in taps:
                    start = r0 + first + k - a
                    term = win_ref[start:start + rows, ls] * w_ref[k:k + 1, ls]
                    z = term if z is None else z + term
                acc = acc + z[a:a + CONV_ROWS, :]
            y_ref[r0:r0 + CONV_ROWS, ls] = acc
    y = y_ref[...]
    mu = jnp.mean(y, axis=-1, keepdims=True)
    yc = y - mu
    var = jnp.mean(yc * yc, axis=-1, keepdims=True)
    z = yc * lax.rsqrt(var + LN_EPS) * g_ref[...] + beta_ref[...]
    o_ref[...] = (z * jax.nn.sigmoid(z)).astype(o_ref.dtype)


def _conv_module(u, conv_w, conv_b, ln_g, ln_b, layer, batch):
    t, width = u.shape
    seq = t // batch
    tiles = seq // CONV_TILE
    halo_per_tile = CONV_TILE // CONV_HALO
    vec = pl.BlockSpec((None, 1, width), lambda b, s: (layer, 0, 0))
    return pl.pallas_call(
        _conv_kernel,
        grid=(batch, tiles),
        in_specs=[pl.BlockSpec((CONV_HALO, width),
                               lambda b, s: (jnp.maximum((b * tiles + s) * halo_per_tile - 1, 0), 0)),
                  pl.BlockSpec((CONV_TILE, width), lambda b, s: (b * tiles + s, 0)),
                  pl.BlockSpec((None, CONV_K, width), lambda b, s: (layer, 0, 0)),
                  vec, vec, vec],
        out_specs=pl.BlockSpec((CONV_TILE, width), lambda b, s: (b * tiles + s, 0)),
        out_shape=jax.ShapeDtypeStruct((t, width), BF16),
        scratch_shapes=[pltpu.VMEM((CONV_HALO + CONV_TILE, width), F32), pltpu.VMEM((CONV_TILE, width), F32)],
        compiler_params=_params(),
        name="conformer_conv",
    )(u, u, conv_w, conv_b, ln_g, ln_b)


def _layer_norm(y, g, b):
    mu = jnp.mean(y, axis=-1, keepdims=True)
    yc = y - mu
    var = jnp.mean(yc * yc, axis=-1, keepdims=True)
    return yc * lax.rsqrt(var + LN_EPS) * g + b


def _route(logits):
    lane = lax.broadcasted_iota(jnp.int32, logits.shape, 1)
    is_group = lane < N_GROUPS
    gl = jnp.where(is_group, logits, -jnp.inf)
    gmax = jnp.max(gl, axis=-1, keepdims=True)
    g_top = jnp.min(jnp.where(gl == gmax, lane, ROUTER_LANES), axis=-1, keepdims=True)
    p_group = 1.0 / jnp.sum(jnp.where(is_group, jnp.exp(gl - gmax), 0.0), axis=-1, keepdims=True)
    e_idx = lane - N_GROUPS
    in_group = (e_idx >= 0) & (e_idx < N_EXPERTS) & ((e_idx // EXPERTS_PER_GROUP) == g_top)
    el = jnp.where(in_group, logits, -jnp.inf)
    m1 = jnp.max(el, axis=-1, keepdims=True)
    i1 = jnp.min(jnp.where(el == m1, lane, ROUTER_LANES), axis=-1, keepdims=True)
    el2 = jnp.where(lane == i1, -jnp.inf, el)
    m2 = jnp.max(el2, axis=-1, keepdims=True)
    i2 = jnp.min(jnp.where(el2 == m2, lane, ROUTER_LANES), axis=-1, keepdims=True)
    e2 = jnp.exp(m2 - m1)
    w1 = p_group / (1.0 + e2)
    w2 = p_group * e2 / (1.0 + e2)
    return i1, i2, w1, w2


def _outproj_kernel(attn_ref, conv_ref, w_ref, x_ref, g_ref, b_ref, wr_ref, br_ref,
                    h_ref, ri_ref, rw_ref, cnt_ref, carry_ref, wb_ref, *, alpha):
    i = pl.program_id(0)
    _cast_weights_once(i, [(w_ref, wb_ref)])

    @pl.when(i == 0)
    def _():
        carry_ref[...] = jnp.zeros_like(carry_ref)

    ka = attn_ref.shape[1]
    mix = jnp.dot(attn_ref[...], wb_ref[0:ka, :], preferred_element_type=F32)
    mix = mix + jnp.dot(conv_ref[...], wb_ref[ka:, :], preferred_element_type=F32)
    h = _layer_norm(alpha * x_ref[...] + mix, g_ref[...], b_ref[...])
    _store_row_chunks(h_ref, h)
    h_hi = h.astype(BF16)
    h_lo = (h - h_hi.astype(F32)).astype(BF16)
    t_hi = jnp.dot(h_hi, wr_ref[...], preferred_element_type=F32)
    t_lo = jnp.dot(h_lo, wr_ref[:, 0:ROUTER_LANES], preferred_element_type=F32)
    logits = t_hi[:, 0:ROUTER_LANES] + t_hi[:, ROUTER_LANES:] + t_lo + br_ref[...]
    i1, i2, w1, w2 = _route(logits)

    tm = logits.shape[0]
    lane = lax.broadcasted_iota(jnp.int32, logits.shape, 1)
    onehot = jnp.where((lane == i1) | (lane == i2), 1.0, 0.0)
    rr = lax.broadcasted_iota(jnp.int32, (tm, tm), 0)
    cc = lax.broadcasted_iota(jnp.int32, (tm, tm), 1)
    earlier = jnp.where(cc < rr, 1.0, 0.0).astype(BF16)
    before = jnp.dot(earlier, onehot.astype(BF16), preferred_element_type=F32) + carry_ref[...]
    r1 = jnp.sum(jnp.where(lane == i1, before, 0.0), axis=-1, keepdims=True)
    r2 = jnp.sum(jnp.where(lane == i2, before, 0.0), axis=-1, keepdims=True)
    carry_ref[...] += jnp.sum(onehot, axis=0, keepdims=True)
    cnt_ref[...] = jnp.broadcast_to(carry_ref[...], cnt_ref.shape)
    rec = jnp.where(lane == 0, (i1 - N_GROUPS).astype(F32),
                    jnp.where(lane == 1, (i2 - N_GROUPS).astype(F32),
                              jnp.where(lane == 2, r1, jnp.where(lane == 3, r2, 0.0))))
    ri_ref[...] = rec.T[0:ri_ref.shape[0], :].astype(jnp.int32)
    rw_ref[...] = jnp.where(lane == 0, w1, jnp.where(lane == 1, w2, 0.0))


def _outproj(attn, conv, w_out, x, ln_g, ln_b, w_router, b_router, layer, alpha, tm=256):
    t, d = x.shape
    ka, kc = attn.shape[1], conv.shape[1]
    vec = pl.BlockSpec((None, 1, d), lambda i: (layer, 0, 0))
    row = lambda w: pl.BlockSpec((tm, w), lambda i: (i, 0))
    return pl.pallas_call(
        functools.partial(_outproj_kernel, alpha=alpha),
        grid=(t // tm,),
        in_specs=[row(ka), row(kc),
                  pl.BlockSpec((None, ka + kc, d), lambda i: (layer, 0, 0), pipeline_mode=pl.Buffered(1)),
                  row(d), vec, vec,
                  pl.BlockSpec((None, d, 2 * ROUTER_LANES), lambda i: (layer, 0, 0)),
                  pl.BlockSpec((None, 1, ROUTER_LANES), lambda i: (layer, 0, 0))],
        out_specs=[pl.BlockSpec((tm * d // LANES, LANES), lambda i: (i, 0)),
                   pl.BlockSpec((SUBLANES, tm), lambda i: (0, i)), row(ROUTER_LANES),
                   pl.BlockSpec((8, ROUTER_LANES), lambda i: (0, 0))],
        out_shape=[jax.ShapeDtypeStruct((t * d // LANES, LANES), F32),
                   jax.ShapeDtypeStruct((SUBLANES, t), jnp.int32),
                   jax.ShapeDtypeStruct((t, ROUTER_LANES), F32),
                   jax.ShapeDtypeStruct((8, ROUTER_LANES), F32)],
        scratch_shapes=[pltpu.VMEM((1, ROUTER_LANES), F32), pltpu.VMEM((ka + kc, d), BF16)],
        compiler_params=_params(dimension_semantics=("arbitrary",)),
        name="outproj_ln_router",
    )(attn, conv, w_out, x, ln_g, ln_b, w_router, b_router)


MOE_TILE = 256


def _moe_plan(route_i, cnt, n_rows):
    e = route_i[0:2, :]
    rank = route_i[2:4, :]
    counts = cnt[0, N_GROUPS:N_GROUPS + N_EXPERTS].astype(jnp.int32)
    ends = jnp.cumsum(counts)
    offs = ends - counts
    start = functools.reduce(jnp.add, [jnp.where(e == k, offs[k], 0) for k in range(N_EXPERTS)])
    pos = (start + rank).reshape(-1)
    n_tiles = n_rows // MOE_TILE
    first_tile = offs // MOE_TILE
    n_items_e = jnp.where(counts > 0, (ends - 1) // MOE_TILE - first_tile + 1, 0)
    item_end = jnp.cumsum(n_items_e)
    item_start = item_end - n_items_e
    n_items = n_tiles + N_EXPERTS - 1
    j = jnp.arange(n_items, dtype=jnp.int32)
    valid = j < item_end[-1]
    jj = jnp.clip(j, 0, jnp.maximum(item_end[-1] - 1, 0))
    ex = jnp.minimum(jnp.sum((jj[:, None] >= item_end[None, :]).astype(jnp.int32), axis=1), N_EXPERTS - 1)
    tile = first_tile[ex] + (jj - item_start[ex])
    lo = jnp.where(valid, jnp.maximum(offs[ex], tile * MOE_TILE) - tile * MOE_TILE, 0)
    hi = jnp.where(valid, jnp.minimum(ends[ex], (tile + 1) * MOE_TILE) - tile * MOE_TILE, 0)
    return pos, tile.astype(jnp.int32), ex.astype(jnp.int32), lo.astype(jnp.int32), hi.astype(jnp.int32)


def _pack_bf16_pair(y):
    n = y.shape[1] // 2
    hi = lax.bitcast_convert_type(y[:, :n].astype(BF16).astype(F32), jnp.uint32)
    lo = lax.bitcast_convert_type(y[:, n:].astype(BF16).astype(F32), jnp.uint32)
    return hi | (lo >> 16)


def _unpack_bf16_pair(u):
    hi = lax.bitcast_convert_type(u & jnp.uint32(0xFFFF0000), F32)
    lo = lax.bitcast_convert_type(u << 16, F32)
    return hi, lo


def _store_row_chunks(ref, val):
    rows, width = val.shape
    n = width // LANES
    for c in range(n):
        ref[pl.ds(c, rows, stride=n), :] = val[:, c * LANES:(c + 1) * LANES]


def _start_alternating(copies):
    for k, cp in enumerate(copies):
        cp.start(priority=k % 2)


def _load_row_chunks(ref, rows, n):
    return jnp.concatenate([ref[pl.ds(c, rows, stride=n), :] for c in range(n)], axis=1)


def _gmm_kernel(pos_ref, tile_ref, exp_ref, lo_ref, hi_ref, h_ref, wgu_ref, wd_ref, ys_ref,
                inv_ref, xbuf, acc_ref, wgu_b_ref, wd_b_ref, sems):
    j = pl.program_id(0)
    n_items = pl.num_programs(0)
    tm = acc_ref.shape[0]
    chunks = acc_ref.shape[1] // LANES
    tokens = pos_ref.shape[0] // 2
    cur = tile_ref[j]
    nxt = tile_ref[jnp.minimum(j + 1, n_items - 1)]
    first = (j == 0) | (cur != tile_ref[jnp.maximum(j - 1, 0)])
    last = (j == n_items - 1) | (nxt != cur)

    def row_copies(tile):
        slot = tile % 2
        return [pltpu.make_async_copy(h_ref.at[pl.ds(inv_ref[tile * tm + r] * chunks, chunks), :],
                                      xbuf.at[slot, pl.ds(r * chunks, chunks), :], sems.at[slot])
                for r in range(tm)]

    @pl.when(j == 0)
    def _():
        def fill(t, carry):
            inv_ref[pos_ref[t]] = t
            inv_ref[pos_ref[tokens + t]] = t
            return carry
        lax.fori_loop(0, tokens, fill, 0, unroll=8)
        _start_alternating(row_copies(cur))

    @pl.when((j < n_items - 1) & (nxt != cur))
    def _():
        _start_alternating(row_copies(nxt))

    @pl.when(first)
    def _():
        for cp in row_copies(cur):
            cp.wait()

    @pl.when((j == 0) | (exp_ref[j] != exp_ref[jnp.maximum(j - 1, 0)]))
    def _():
        wgu_b_ref[...] = wgu_ref[...].astype(BF16)
        wd_b_ref[...] = wd_ref[...].astype(BF16)

    f = wd_ref.shape[0]
    x = _load_row_chunks(xbuf.at[cur % 2], tm, chunks).astype(BF16)
    hg = jnp.dot(x, wgu_b_ref[...], preferred_element_type=F32)
    gate = hg[:, :f]
    row = lax.broadcasted_iota(jnp.int32, gate.shape, 0)
    mine = (row >= lo_ref[j]) & (row < hi_ref[j])
    act = jnp.where(mine, gate * jax.nn.sigmoid(gate) * hg[:, f:], 0.0)
    y = jnp.dot(act.astype(BF16), wd_b_ref[...], preferred_element_type=F32)

    @pl.when(first)
    def _():
        acc_ref[...] = y

    @pl.when(jnp.logical_not(first))
    def _():
        acc_ref[...] += y

    @pl.when(last)
    def _():
        _store_row_chunks(ys_ref, _pack_bf16_pair(acc_ref[...]))


def _gmm(plan, h_chunks, w_gu, w_down, layer):
    pos, tile, ex, lo, hi = plan
    n_rows = pos.shape[0]
    d = w_gu.shape[2]
    f2 = w_gu.shape[-1]
    f = w_down.shape[2]
    chunks = d // LANES
    return pl.pallas_call(
        _gmm_kernel,
        grid_spec=pltpu.PrefetchScalarGridSpec(
            num_scalar_prefetch=5,
            grid=(tile.shape[0],),
            in_specs=[pl.BlockSpec(memory_space=pl.ANY),
                      pl.BlockSpec((None, None, d, f2), lambda j, pos, tile, ex, lo, hi: (layer, ex[j], 0, 0)),
                      pl.BlockSpec((None, None, f, d), lambda j, pos, tile, ex, lo, hi: (layer, ex[j], 0, 0))],
            out_specs=pl.BlockSpec((MOE_TILE * chunks // 2, LANES),
                                   lambda j, pos, tile, ex, lo, hi: (tile[j], 0)),
            scratch_shapes=[pltpu.SMEM((n_rows,), jnp.int32),
                            pltpu.VMEM((2, MOE_TILE * chunks, LANES), F32),
                            pltpu.VMEM((MOE_TILE, d), F32),
                            pltpu.VMEM((d, f2), BF16),
                            pltpu.VMEM((f, d), BF16),
                            pltpu.SemaphoreType.DMA((2,))]),
        out_shape=jax.ShapeDtypeStruct((n_rows * chunks // 2, LANES), jnp.uint32),
        compiler_params=_params(dimension_semantics=("arbitrary",)),
        name="moe_grouped_matmul",
    )(pos, tile, ex, lo, hi, h_chunks, w_gu, w_down)


def _ple_kernel(pos_ref, h_ref, rw_ref, ys_ref, g_ref, b_ref, wgate_ref, p_ref, wp_ref, x_ref, xb_ref,
                ybuf, sems, wgate_b_ref, wp_b_ref, *, alpha):
    i = pl.program_id(0)
    n_tiles = pl.num_programs(0)
    tm = rw_ref.shape[0]
    tokens = pos_ref.shape[0] // 2
    yc = ybuf.shape[2] // tm
    hc = h_ref.shape[0] // tm

    def row_copies(tile, slot):
        return [pltpu.make_async_copy(ys_ref.at[pl.ds(pos_ref[s * tokens + tile * tm + r] * yc, yc), :],
                                      ybuf.at[slot, s, pl.ds(r * yc, yc), :], sems.at[slot])
                for r in range(tm) for s in range(2)]

    @pl.when(i == 0)
    def _():
        wgate_b_ref[...] = wgate_ref[...].astype(BF16)
        wp_b_ref[...] = wp_ref[...].astype(BF16)
        _start_alternating(row_copies(0, 0))

    slot = i % 2
    ahead = jnp.minimum(i + 1, n_tiles - 1)
    _start_alternating(row_copies(ahead, 1 - slot))
    for cp in row_copies(i, slot):
        cp.wait()
    rw = rw_ref[...]
    y0_hi, y0_lo = _unpack_bf16_pair(_load_row_chunks(ybuf.at[slot, 0], tm, yc))
    y1_hi, y1_lo = _unpack_bf16_pair(_load_row_chunks(ybuf.at[slot, 1], tm, yc))
    w0, w1 = rw[:, 0:1], rw[:, 1:2]
    f = jnp.concatenate([w0 * y0_hi + w1 * y1_hi, w0 * y0_lo + w1 * y1_lo], axis=1)
    h2 = _layer_norm(alpha * _load_row_chunks(h_ref, tm, hc) + f, g_ref[...], b_ref[...])
    gate = jax.nn.sigmoid(jnp.dot(h2.astype(BF16), wgate_b_ref[...], preferred_element_type=F32))
    pe = jnp.dot(p_ref[...].astype(BF16), wp_b_ref[...], preferred_element_type=F32)
    x_new = h2 + gate * pe
    x_ref[...] = x_new
    xb_ref[...] = x_new.astype(BF16)

    @pl.when(i == n_tiles - 1)
    def _():
        for cp in row_copies(ahead, 1 - slot):
            cp.wait()


def _ple(pos, h_chunks, rw, ys, ln_g, ln_b, w_gate, p, w_ple, layer, alpha, tm=256):
    d = w_gate.shape[-1]
    t = h_chunks.shape[0] * LANES // d
    pd = p.shape[-1]
    vec = pl.BlockSpec((None, 1, d), lambda i, pos: (layer, 0, 0))
    row = lambda w: pl.BlockSpec((tm, w), lambda i, pos: (i, 0))
    return pl.pallas_call(
        functools.partial(_ple_kernel, alpha=alpha),
        grid_spec=pltpu.PrefetchScalarGridSpec(
            num_scalar_prefetch=1,
            grid=(t // tm,),
            in_specs=[pl.BlockSpec((tm * d // LANES, LANES), lambda i, pos: (i, 0)),
                      row(ROUTER_LANES), pl.BlockSpec(memory_space=pl.ANY), vec, vec,
                      pl.BlockSpec((None, d, d), lambda i, pos: (layer, 0, 0), pipeline_mode=pl.Buffered(1)),
                      pl.BlockSpec((None, tm, pd), lambda i, pos: (layer, i, 0)),
                      pl.BlockSpec((None, pd, d), lambda i, pos: (layer, 0, 0), pipeline_mode=pl.Buffered(1))],
            out_specs=[row(d), row(d)],
            scratch_shapes=[pltpu.VMEM((2, 2, tm * d // (2 * LANES), LANES), jnp.uint32),
                            pltpu.SemaphoreType.DMA((2,)),
                            pltpu.VMEM((d, d), BF16), pltpu.VMEM((pd, d), BF16)]),
        out_shape=[jax.ShapeDtypeStruct((t, d), F32), jax.ShapeDtypeStruct((t, d), BF16)],
        compiler_params=_params(dimension_semantics=("arbitrary",)),
        name="gather_ln2_ple",
    )(pos, h_chunks, rw, ys, ln_g, ln_b, w_gate, p, w_ple)


def kernel(x, p, w_in, conv_w, conv_b, conv_ln_g, conv_ln_b, w_out, rel_bias, ln1_g, ln1_b, router_g_w, router_g_b, router_e_w, router_e_b, expert_w_gu, expert_w_down, ln2_g, ln2_b, ple_w, ple_gate_w):
    batch, seq, d = x.shape
    depth = w_in.shape[0]
    n_heads = rel_bias.shape[1]
    attn_w = n_heads * HEAD_DIM
    conv_width = conv_w.shape[2]
    t = batch * seq
    alpha = (2 * depth) ** 0.25

    pad = ROUTER_LANES - N_GROUPS - N_EXPERTS
    w_router = jnp.pad(jnp.concatenate([router_g_w, router_e_w], axis=-1), ((0, 0), (0, 0), (0, pad)))
    w_router_hi = w_router.astype(BF16)
    w_router_lo = (w_router - w_router_hi.astype(F32)).astype(BF16)
    w_router = jnp.concatenate([w_router_hi, w_router_lo], axis=-1)
    b_router = jnp.pad(jnp.concatenate([router_g_b, router_e_b], axis=-1), ((0, 0), (0, pad)))[:, None, :]
    vec3 = lambda v: v[:, None, :]
    p2 = p.reshape(depth, t, p.shape[-1])

    bias = _bias_tiles(rel_bias)
    xf = x.reshape(t, d)
    xb = xf.astype(BF16)
    for i in range(depth):
        qk = _proj_qk(xb, w_in, i, tn=attn_w)
        vt = _proj_vt(xb, w_in, i, batch, attn_w, col_block=2)
        u = _proj_glu(xb, w_in, i, a_col=3 * attn_w, width=conv_width)
        attn = _attention(qk, vt, bias, batch, n_heads)
        conv = _conv_module(u, conv_w, vec3(conv_b), vec3(conv_ln_g), vec3(conv_ln_b), i, batch)
        h, route_i, route_w, cnt = _outproj(attn, conv, w_out, xf, vec3(ln1_g), vec3(ln1_b),
                                            w_router, b_router, i, alpha)
        plan = _moe_plan(route_i, cnt, 2 * t)
        ys = _gmm(plan, h, expert_w_gu, expert_w_down, i)
        xf, xb = _ple(plan[0], h, route_w, ys, vec3(ln2_g), vec3(ln2_b), ple_gate_w, p2, ple_w, i, alpha)
    return xf.reshape(batch, seq, d)
```

```python
import functools
import math

import numpy as np
import jax
import jax.numpy as jnp
from jax import lax
from jax.experimental import pallas as pl
from jax.experimental.pallas import tpu as pltpu

F32 = jnp.float32
BF16 = jnp.bfloat16

HEAD_DIM = 128
BLOCK = 256
TOP_BLOCKS = 3
CONV_K = 31
N_BUCKETS = 32
MAX_DISTANCE = 128
N_GROUPS = 4
EXPERTS_PER_GROUP = 8
N_EXPERTS = N_GROUPS * EXPERTS_PER_GROUP
LN_EPS = 1e-5
LOG2_E = math.log2(math.e)
NEG = -1e30
ROUTER_LANES = 128
VMEM_LIMIT = 56 * 1024 * 1024


def _bucket_thresholds():
    n = np.arange(0, 4 * MAX_DISTANCE)
    max_exact = N_BUCKETS // 2
    nf = np.maximum(n, max_exact).astype(np.float32)
    large = max_exact + (np.log(nf / np.float32(max_exact)) / np.float32(math.log(MAX_DISTANCE / max_exact))
                         * np.float32(N_BUCKETS - max_exact)).astype(np.int32)
    bucket = np.where(n < max_exact, n, np.minimum(large, N_BUCKETS - 1))
    assert np.all(np.diff(bucket) >= 0) and bucket[-1] == N_BUCKETS - 1
    return [int(np.argmax(bucket >= b)) for b in range(N_BUCKETS)]


_BUCKET_START = _bucket_thresholds()


def _params(**kw):
    return pltpu.CompilerParams(vmem_limit_bytes=VMEM_LIMIT, **kw)


def _bias_kernel(rb_ref, out_ref):
    h = pl.program_id(0)
    kj = lax.broadcasted_iota(jnp.int32, (BLOCK, BLOCK), 0)
    qi = lax.broadcasted_iota(jnp.int32, (BLOCK, BLOCK), 1)
    for d in range(3):
        rel = qi - kj + d * BLOCK
        val = jnp.full((BLOCK, BLOCK), rb_ref[0, h], F32)
        for b in range(1, N_BUCKETS):
            val = jnp.where(rel >= _BUCKET_START[b], rb_ref[b, h], val)
        val = val * LOG2_E
        if d == 0:
            val = jnp.where(rel >= 0, val, NEG)
        out_ref[d] = val


def _bias_tiles(rel_bias):
    n_heads = rel_bias.shape[1]
    assert _BUCKET_START[-1] <= BLOCK + 1
    return pl.pallas_call(
        _bias_kernel,
        grid=(n_heads,),
        in_specs=[pl.BlockSpec(memory_space=pltpu.SMEM)],
        out_specs=pl.BlockSpec((None, 3, BLOCK, BLOCK), lambda h: (h, 0, 0, 0)),
        out_shape=jax.ShapeDtypeStruct((n_heads, 3, BLOCK, BLOCK), F32),
        name="t5_bias_tiles",
    )(rel_bias)


def _cast_weights_once(row_tile_id, pairs):
    @pl.when(row_tile_id == 0)
    def _():
        for src, dst in pairs:
            dst[...] = src[...].astype(BF16)


def _mm_kernel(x_ref, w_ref, o_ref, wb_ref):
    _cast_weights_once(pl.program_id(1), [(w_ref, wb_ref)])
    o_ref[...] = jnp.dot(x_ref[...], wb_ref[...], preferred_element_type=F32).astype(o_ref.dtype)


def _proj_qk(xb, w_in, layer, tm=1024, tn=1024):
    t, d = xb.shape
    n_out = 2 * tn
    return pl.pallas_call(
        _mm_kernel,
        grid=(n_out // tn, t // tm),
        in_specs=[pl.BlockSpec((tm, d), lambda j, i: (i, 0)),
                  pl.BlockSpec((None, d, tn), lambda j, i: (layer, 0, j))],
        out_specs=pl.BlockSpec((tm, tn), lambda j, i: (i, j)),
        out_shape=jax.ShapeDtypeStruct((t, n_out), BF16),
        scratch_shapes=[pltpu.VMEM((d, tn), BF16)],
        compiler_params=_params(dimension_semantics=("arbitrary", "arbitrary")),
        name="proj_qk",
    )(xb, w_in)


def _proj_vt_kernel(x_ref, w_ref, o_ref, wb_ref):
    _cast_weights_once(pl.program_id(0), [(w_ref, wb_ref)])
    r = jnp.dot(x_ref[...], wb_ref[...], preferred_element_type=F32)
    o_ref[...] = r.T.astype(o_ref.dtype)


def _proj_vt(xb, w_in, layer, batch, width, col_block, tm=512):
    t, d = xb.shape
    seq = t // batch
    per_b = seq // tm
    return pl.pallas_call(
        _proj_vt_kernel,
        grid=(t // tm,),
        in_specs=[pl.BlockSpec((tm, d), lambda i: (i, 0)),
                  pl.BlockSpec((None, d, width), lambda i: (layer, 0, col_block))],
        out_specs=pl.BlockSpec((None, width, tm), lambda i: (i // per_b, 0, i % per_b)),
        out_shape=jax.ShapeDtypeStruct((batch, width, seq), BF16),
        scratch_shapes=[pltpu.VMEM((d, width), BF16)],
        compiler_params=_params(dimension_semantics=("arbitrary",)),
        name="proj_vt",
    )(xb, w_in)


def _proj_glu_kernel(x_ref, wa_ref, wg_ref, o_ref, wab_ref, wgb_ref):
    _cast_weights_once(pl.program_id(1), [(wa_ref, wab_ref), (wg_ref, wgb_ref)])
    x = x_ref[...]
    a = jnp.dot(x, wab_ref[...], preferred_element_type=F32)
    g = jnp.dot(x, wgb_ref[...], preferred_element_type=F32)
    o_ref[...] = a * jax.nn.sigmoid(g)


def _proj_glu(xb, w_in, layer, a_col, width, tm=1024, tn=512):
    t, d = xb.shape
    a_blk = a_col // tn
    g_blk = (a_col + width) // tn
    return pl.pallas_call(
        _proj_glu_kernel,
        grid=(width // tn, t // tm),
        in_specs=[pl.BlockSpec((tm, d), lambda j, i: (i, 0)),
                  pl.BlockSpec((None, d, tn), lambda j, i: (layer, 0, a_blk + j)),
                  pl.BlockSpec((None, d, tn), lambda j, i: (layer, 0, g_blk + j))],
        out_specs=pl.BlockSpec((tm, tn), lambda j, i: (i, j)),
        out_shape=jax.ShapeDtypeStruct((t, width), F32),
        scratch_shapes=[pltpu.VMEM((d, tn), BF16), pltpu.VMEM((d, tn), BF16)],
        compiler_params=_params(dimension_semantics=("arbitrary", "arbitrary")),
        name="proj_glu",
    )(xb, w_in, w_in)


ATTN_HEADS_PER_STEP = 4
_NT = (((1,), (1,)), ((), ()))


def _block_penalty(kmean, q, c):
    gate = lax.dot_general(kmean, q.astype(F32), _NT, precision=lax.Precision.HIGHEST,
                           preferred_element_type=F32)
    row = lax.broadcasted_iota(jnp.int32, gate.shape, 0)
    gate_m = jnp.where(row < c, gate, -jnp.inf)
    pens = []
    for n in range(c):
        g_n = gate[n:n + 1, :]
        beats = (gate_m > g_n) | ((gate_m == g_n) & (row < n))
        rank = jnp.sum(beats.astype(F32), axis=0, keepdims=True)
        pens.append(jnp.where(rank < TOP_BLOCKS, 0.0, NEG))
    return pens


def _attn_row(c, hh, q_ref, k_ref, vt_ref, bias_ref, o_ref, kmean_ref, logit_ref, prob_ref, scale):
    hs = slice(hh * HEAD_DIM, (hh + 1) * HEAD_DIM)
    q = q_ref[:, hs]
    keys = (c + 1) * BLOCK
    groups = BLOCK // SUBLANES
    pens = _block_penalty(kmean_ref[hh], q, c) if c > TOP_BLOCKS else None
    m8 = None
    for n in range(c + 1):
        s = lax.dot_general(k_ref[n * BLOCK:(n + 1) * BLOCK, hs], q, _NT, preferred_element_type=F32)
        b = s * scale + bias_ref[hh, min(c - n, 2)]
        if pens is not None and n < c:
            b = b + pens[n]
        logit_ref[hh, n] = b
        bm = jnp.max(b.reshape(groups, SUBLANES, BLOCK), axis=0)
        m8 = bm if m8 is None else jnp.maximum(m8, bm)
    m = jnp.max(m8, axis=0, keepdims=True)
    l8 = jnp.zeros((SUBLANES, BLOCK), F32)
    for n in range(c + 1):
        p = jnp.exp2(logit_ref[hh, n] - m)
        l8 = l8 + jnp.sum(p.reshape(groups, SUBLANES, BLOCK), axis=0)
        prob_ref[hh, n * BLOCK:(n + 1) * BLOCK, :] = p.astype(BF16)
    l = jnp.sum(l8, axis=0, keepdims=True)
    acc = jnp.dot(vt_ref[hs, 0:keys], prob_ref[hh, 0:keys, :], preferred_element_type=F32)
    o_ref[:, hs] = (acc / l).T.astype(o_ref.dtype)


def _attn_kernel(q_ref, k_ref, vt_ref, bias_ref, o_ref, kmean_ref, logit_ref, prob_ref, *, n_blocks, scale):
    c_id = pl.program_id(2)
    heads = q_ref.shape[1] // HEAD_DIM

    @pl.when(c_id == 0)
    def _():
        for hh in range(heads):
            for n in range(n_blocks):
                kb = k_ref[n * BLOCK:(n + 1) * BLOCK, hh * HEAD_DIM:(hh + 1) * HEAD_DIM]
                kmean_ref[hh, n:n + 1, :] = jnp.mean(kb.astype(F32), axis=0, keepdims=True)

    for c in range(n_blocks):
        @pl.when(c_id == c)
        def _(c=c):
            for hh in range(heads):
                _attn_row(c, hh, q_ref, k_ref, vt_ref, bias_ref, o_ref, kmean_ref, logit_ref, prob_ref, scale)


def _attention(qk, vt, bias, batch, n_heads):
    t = qk.shape[0]
    seq = t // batch
    nb = seq // BLOCK
    hp = ATTN_HEADS_PER_STEP
    w = hp * HEAD_DIM
    kern = functools.partial(_attn_kernel, n_blocks=nb, scale=HEAD_DIM ** -0.5 * LOG2_E)
    return pl.pallas_call(
        kern,
        grid=(batch, n_heads // hp, nb),
        in_specs=[pl.BlockSpec((BLOCK, w), lambda b, h, c: (b * nb + c, h)),
                  pl.BlockSpec((seq, w), lambda b, h, c: (b, n_heads // hp + h)),
                  pl.BlockSpec((None, w, seq), lambda b, h, c: (b, h, 0)),
                  pl.BlockSpec((hp, 3, BLOCK, BLOCK), lambda b, h, c: (h, 0, 0, 0))],
        out_specs=pl.BlockSpec((BLOCK, w), lambda b, h, c: (b * nb + c, h)),
        out_shape=jax.ShapeDtypeStruct((t, n_heads * HEAD_DIM), BF16),
        scratch_shapes=[pltpu.VMEM((hp, nb, HEAD_DIM), F32),
                        pltpu.VMEM((hp, nb, BLOCK, BLOCK), F32),
                        pltpu.VMEM((hp, seq, BLOCK), BF16)],
        compiler_params=_params(),
        name="moba_attention",
    )(qk, qk, vt, bias)


CONV_TILE = 256
CONV_HALO = 32
CONV_ROWS = 64
LANES = 128
SUBLANES = 8


def _conv_kernel(prev_ref, cur_ref, w_ref, b_ref, g_ref, beta_ref, o_ref, win_ref, y_ref):
    s = pl.program_id(1)
    win_ref[0:CONV_HALO, :] = jnp.where(s > 0, prev_ref[...], 0.0)
    win_ref[CONV_HALO:, :] = cur_ref[...]
    width = cur_ref.shape[1]
    first = CONV_HALO - (CONV_K - 1)
    phases = [[k for k in range(CONV_K) if (first + k) % SUBLANES == a] for a in range(SUBLANES)]
    for lc in range(width // LANES):
        ls = slice(lc * LANES, (lc + 1) * LANES)
        for rc in range(CONV_TILE // CONV_ROWS):
            r0 = rc * CONV_ROWS
            acc = jnp.broadcast_to(b_ref[:, ls], (CONV_ROWS, LANES))
            for a, taps in enumerate(phases):
                rows = CONV_ROWS + (SUBLANES if a else 0)
                z = None
                for k in taps:
                    start = r0 + first + k - a
                    term = win_ref[start:start + rows, ls] * w_ref[k:k + 1, ls]
                    z = term if z is None else z + term
                acc = acc + z[a:a + CONV_ROWS, :]
            y_ref[r0:r0 + CONV_ROWS, ls] = acc
    y = y_ref[...]
    mu = jnp.mean(y, axis=-1, keepdims=True)
    yc = y - mu
    var = jnp.mean(yc * yc, axis=-1, keepdims=True)
    z = yc * lax.rsqrt(var + LN_EPS) * g_ref[...] + beta_ref[...]
    o_ref[...] = (z * jax.nn.sigmoid(z)).astype(o_ref.dtype)


def _conv_module(u, conv_w, conv_b, ln_g, ln_b, layer, batch):
    t, width = u.shape
    seq = t // batch
    tiles = seq // CONV_TILE
    halo_per_tile = CONV_TILE // CONV_HALO
    vec = pl.BlockSpec((None, 1, width), lambda b, s: (layer, 0, 0))
    return pl.pallas_call(
        _conv_kernel,
        grid=(batch, tiles),
        in_specs=[pl.BlockSpec((CONV_HALO, width),
                               lambda b, s: (jnp.maximum((b * tiles + s) * halo_per_tile - 1, 0), 0)),
                  pl.BlockSpec((CONV_TILE, width), lambda b, s: (b * tiles + s, 0)),
                  pl.BlockSpec((None, CONV_K, width), lambda b, s: (layer, 0, 0)),
                  vec, vec, vec],
        out_specs=pl.BlockSpec((CONV_TILE, width), lambda b, s: (b * tiles + s, 0)),
        out_shape=jax.ShapeDtypeStruct((t, width), BF16),
        scratch_shapes=[pltpu.VMEM((CONV_HALO + CONV_TILE, width), F32), pltpu.VMEM((CONV_TILE, width), F32)],
        compiler_params=_params(),
        name="conformer_conv",
    )(u, u, conv_w, conv_b, ln_g, ln_b)


def _layer_norm(y, g, b):
    mu = jnp.mean(y, axis=-1, keepdims=True)
    yc = y - mu
    var = jnp.mean(yc * yc, axis=-1, keepdims=True)
    return yc * lax.rsqrt(var + LN_EPS) * g + b


def _route(logits):
    lane = lax.broadcasted_iota(jnp.int32, logits.shape, 1)
    is_group = lane < N_GROUPS
    gl = jnp.where(is_group, logits, -jnp.inf)
    gmax = jnp.max(gl, axis=-1, keepdims=True)
    g_top = jnp.min(jnp.where(gl == gmax, lane, ROUTER_LANES), axis=-1, keepdims=True)
    p_group = 1.0 / jnp.sum(jnp.where(is_group, jnp.exp(gl - gmax), 0.0), axis=-1, keepdims=True)
    e_idx = lane - N_GROUPS
    in_group = (e_idx >= 0) & (e_idx < N_EXPERTS) & ((e_idx // EXPERTS_PER_GROUP) == g_top)
    el = jnp.where(in_group, logits, -jnp.inf)
    m1 = jnp.max(el, axis=-1, keepdims=True)
    i1 = jnp.min(jnp.where(el == m1, lane, ROUTER_LANES), axis=-1, keepdims=True)
    el2 = jnp.where(lane == i1, -jnp.inf, el)
    m2 = jnp.max(el2, axis=-1, keepdims=True)
    i2 = jnp.min(jnp.where(el2 == m2, lane, ROUTER_LANES), axis=-1, keepdims=True)
    e2 = jnp.exp(m2 - m1)
    w1 = p_group / (1.0 + e2)
    w2 = p_group * e2 / (1.0 + e2)
    return i1, i2, w1, w2


def _outproj_kernel(attn_ref, conv_ref, w_ref, x_ref, g_ref, b_ref, wr_ref, br_ref,
                    h_ref, hp_ref, ri_ref, rw_ref, cnt_ref, carry_ref, wb_ref, *, alpha):
    i = pl.program_id(0)
    _cast_weights_once(i, [(w_ref, wb_ref)])

    @pl.when(i == 0)
    def _():
        carry_ref[...] = jnp.zeros_like(carry_ref)

    ka = attn_ref.shape[1]
    mix = jnp.dot(attn_ref[...], wb_ref[0:ka, :], preferred_element_type=F32)
    mix = mix + jnp.dot(conv_ref[...], wb_ref[ka:, :], preferred_element_type=F32)
    h = _layer_norm(alpha * x_ref[...] + mix, g_ref[...], b_ref[...])
    _store_row_chunks(h_ref, h)
    _store_row_chunks(hp_ref, _pack_bf16_pair(h))
    h_hi = h.astype(BF16)
    h_lo = (h - h_hi.astype(F32)).astype(BF16)
    t_hi = jnp.dot(h_hi, wr_ref[...], preferred_element_type=F32)
    t_lo = jnp.dot(h_lo, wr_ref[:, 0:ROUTER_LANES], preferred_element_type=F32)
    logits = t_hi[:, 0:ROUTER_LANES] + t_hi[:, ROUTER_LANES:] + t_lo + br_ref[...]
    i1, i2, w1, w2 = _route(logits)

    tm = logits.shape[0]
    lane = lax.broadcasted_iota(jnp.int32, logits.shape, 1)
    onehot = jnp.where((lane == i1) | (lane == i2), 1.0, 0.0)
    rr = lax.broadcasted_iota(jnp.int32, (tm, tm), 0)
    cc = lax.broadcasted_iota(jnp.int32, (tm, tm), 1)
    earlier = jnp.where(cc < rr, 1.0, 0.0).astype(BF16)
    before = jnp.dot(earlier, onehot.astype(BF16), preferred_element_type=F32) + carry_ref[...]
    r1 = jnp.sum(jnp.where(lane == i1, before, 0.0), axis=-1, keepdims=True)
    r2 = jnp.sum(jnp.where(lane == i2, before, 0.0), axis=-1, keepdims=True)
    carry_ref[...] += jnp.sum(onehot, axis=0, keepdims=True)
    cnt_ref[...] = jnp.broadcast_to(carry_ref[...], cnt_ref.shape)
    rec = jnp.where(lane == 0, (i1 - N_GROUPS).astype(F32),
                    jnp.where(lane == 1, (i2 - N_GROUPS).astype(F32),
                              jnp.where(lane == 2, r1, jnp.where(lane == 3, r2, 0.0))))
    ri_ref[...] = rec.T[0:ri_ref.shape[0], :].astype(jnp.int32)
    rw_ref[...] = jnp.where(lane == 0, w1, jnp.where(lane == 1, w2, 0.0))


def _outproj(attn, conv, w_out, x, ln_g, ln_b, w_router, b_router, layer, alpha, tm=256):
    t, d = x.shape
    ka, kc = attn.shape[1], conv.shape[1]
    vec = pl.BlockSpec((None, 1, d), lambda i: (layer, 0, 0))
    row = lambda w: pl.BlockSpec((tm, w), lambda i: (i, 0))
    return pl.pallas_call(
        functools.partial(_outproj_kernel, alpha=alpha),
        grid=(t // tm,),
        in_specs=[row(ka), row(kc),
                  pl.BlockSpec((None, ka + kc, d), lambda i: (layer, 0, 0), pipeline_mode=pl.Buffered(1)),
                  row(d), vec, vec,
                  pl.BlockSpec((None, d, 2 * ROUTER_LANES), lambda i: (layer, 0, 0)),
                  pl.BlockSpec((None, 1, ROUTER_LANES), lambda i: (layer, 0, 0))],
        out_specs=[pl.BlockSpec((tm * d // LANES, LANES), lambda i: (i, 0)),
                   pl.BlockSpec((tm * d // (2 * LANES), LANES), lambda i: (i, 0)),
                   pl.BlockSpec((SUBLANES, tm), lambda i: (0, i)), row(ROUTER_LANES),
                   pl.BlockSpec((8, ROUTER_LANES), lambda i: (0, 0))],
        out_shape=[jax.ShapeDtypeStruct((t * d // LANES, LANES), F32),
                   jax.ShapeDtypeStruct((t * d // (2 * LANES), LANES), jnp.uint32),
                   jax.ShapeDtypeStruct((SUBLANES, t), jnp.int32),
                   jax.ShapeDtypeStruct((t, ROUTER_LANES), F32),
                   jax.ShapeDtypeStruct((8, ROUTER_LANES), F32)],
        scratch_shapes=[pltpu.VMEM((1, ROUTER_LANES), F32), pltpu.VMEM((ka + kc, d), BF16)],
        compiler_params=_params(dimension_semantics=("arbitrary",)),
        name="outproj_ln_router",
    )(attn, conv, w_out, x, ln_g, ln_b, w_router, b_router)


MOE_TILE = 256


def _moe_plan(route_i, cnt, n_rows):
    e = route_i[0:2, :]
    rank = route_i[2:4, :]
    counts = cnt[0, N_GROUPS:N_GROUPS + N_EXPERTS].astype(jnp.int32)
    ends = jnp.cumsum(counts)
    offs = ends - counts
    start = functools.reduce(jnp.add, [jnp.where(e == k, offs[k], 0) for k in range(N_EXPERTS)])
    pos = (start + rank).reshape(-1)
    n_tiles = n_rows // MOE_TILE
    first_tile = offs // MOE_TILE
    n_items_e = jnp.where(counts > 0, (ends - 1) // MOE_TILE - first_tile + 1, 0)
    item_end = jnp.cumsum(n_items_e)
    item_start = item_end - n_items_e
    n_items = n_tiles + N_EXPERTS - 1
    j = jnp.arange(n_items, dtype=jnp.int32)
    valid = j < item_end[-1]
    jj = jnp.clip(j, 0, jnp.maximum(item_end[-1] - 1, 0))
    ex = jnp.minimum(jnp.sum((jj[:, None] >= item_end[None, :]).astype(jnp.int32), axis=1), N_EXPERTS - 1)
    tile = first_tile[ex] + (jj - item_start[ex])
    lo = jnp.where(valid, jnp.maximum(offs[ex], tile * MOE_TILE) - tile * MOE_TILE, 0)
    hi = jnp.where(valid, jnp.minimum(ends[ex], (tile + 1) * MOE_TILE) - tile * MOE_TILE, 0)
    return pos, tile.astype(jnp.int32), ex.astype(jnp.int32), lo.astype(jnp.int32), hi.astype(jnp.int32)


def _pack_bf16_pair(y):
    n = y.shape[1] // 2
    hi = lax.bitcast_convert_type(y[:, :n].astype(BF16).astype(F32), jnp.uint32)
    lo = lax.bitcast_convert_type(y[:, n:].astype(BF16).astype(F32), jnp.uint32)
    return hi | (lo >> 16)


def _unpack_bf16_pair(u):
    hi = lax.bitcast_convert_type(u & jnp.uint32(0xFFFF0000), F32)
    lo = lax.bitcast_convert_type(u << 16, F32)
    return hi, lo


def _store_row_chunks(ref, val):
    rows, width = val.shape
    n = width // LANES
    for c in range(n):
        ref[pl.ds(c, rows, stride=n), :] = val[:, c * LANES:(c + 1) * LANES]


def _start_alternating(copies):
    for k, cp in enumerate(copies):
        cp.start(priority=k % 2)


def _load_row_chunks(ref, rows, n):
    return jnp.concatenate([ref[pl.ds(c, rows, stride=n), :] for c in range(n)], axis=1)


def _gmm_kernel(pos_ref, tile_ref, exp_ref, lo_ref, hi_ref, h_ref, wgu_ref, wd_ref, ys_ref,
                inv_ref, xbuf, acc_ref, wgu_b_ref, wd_b_ref, sems):
    j = pl.program_id(0)
    n_items = pl.num_programs(0)
    tm = acc_ref.shape[0]
    chunks = xbuf.shape[1] // tm
    tokens = pos_ref.shape[0] // 2
    cur = tile_ref[j]
    nxt = tile_ref[jnp.minimum(j + 1, n_items - 1)]
    first = (j == 0) | (cur != tile_ref[jnp.maximum(j - 1, 0)])
    last = (j == n_items - 1) | (nxt != cur)

    def row_copies(tile):
        slot = tile % 2
        return [pltpu.make_async_copy(h_ref.at[pl.ds(inv_ref[tile * tm + r] * chunks, chunks), :],
                                      xbuf.at[slot, pl.ds(r * chunks, chunks), :], sems.at[slot])
                for r in range(tm)]

    @pl.when(j == 0)
    def _():
        def fill(t, carry):
            inv_ref[pos_ref[t]] = t
            inv_ref[pos_ref[tokens + t]] = t
            return carry
        lax.fori_loop(0, tokens, fill, 0, unroll=8)
        _start_alternating(row_copies(cur))

    @pl.when((j < n_items - 1) & (nxt != cur))
    def _():
        _start_alternating(row_copies(nxt))

    @pl.when(first)
    def _():
        for cp in row_copies(cur):
            cp.wait()

    @pl.when((j == 0) | (exp_ref[j] != exp_ref[jnp.maximum(j - 1, 0)]))
    def _():
        wgu_b_ref[...] = wgu_ref[...].astype(BF16)
        wd_b_ref[...] = wd_ref[...].astype(BF16)

    f = wd_ref.shape[0]
    x_hi, x_lo = _unpack_bf16_pair(_load_row_chunks(xbuf.at[cur % 2], tm, chunks))
    x = jnp.concatenate([x_hi, x_lo], axis=1).astype(BF16)
    hg = jnp.dot(x, wgu_b_ref[...], preferred_element_type=F32)
    gate = hg[:, :f]
    row = lax.broadcasted_iota(jnp.int32, gate.shape, 0)
    mine = (row >= lo_ref[j]) & (row < hi_ref[j])
    act = jnp.where(mine, gate * jax.nn.sigmoid(gate) * hg[:, f:], 0.0)
    y = jnp.dot(act.astype(BF16), wd_b_ref[...], preferred_element_type=F32)

    @pl.when(first)
    def _():
        acc_ref[...] = y

    @pl.when(jnp.logical_not(first))
    def _():
        acc_ref[...] += y

    @pl.when(last)
    def _():
        _store_row_chunks(ys_ref, _pack_bf16_pair(acc_ref[...]))


def _gmm(plan, h_chunks, w_gu, w_down, layer):
    pos, tile, ex, lo, hi = plan
    n_rows = pos.shape[0]
    d = w_gu.shape[2]
    f2 = w_gu.shape[-1]
    f = w_down.shape[2]
    chunks = d // LANES
    return pl.pallas_call(
        _gmm_kernel,
        grid_spec=pltpu.PrefetchScalarGridSpec(
            num_scalar_prefetch=5,
            grid=(tile.shape[0],),
            in_specs=[pl.BlockSpec(memory_space=pl.ANY),
                      pl.BlockSpec((None, None, d, f2), lambda j, pos, tile, ex, lo, hi: (layer, ex[j], 0, 0)),
                      pl.BlockSpec((None, None, f, d), lambda j, pos, tile, ex, lo, hi: (layer, ex[j], 0, 0))],
            out_specs=pl.BlockSpec((MOE_TILE * chunks // 2, LANES),
                                   lambda j, pos, tile, ex, lo, hi: (tile[j], 0)),
            scratch_shapes=[pltpu.SMEM((n_rows,), jnp.int32),
                            pltpu.VMEM((2, MOE_TILE * chunks // 2, LANES), jnp.uint32),
                            pltpu.VMEM((MOE_TILE, d), F32),
                            pltpu.VMEM((d, f2), BF16),
                            pltpu.VMEM((f, d), BF16),
                            pltpu.SemaphoreType.DMA((2,))]),
        out_shape=jax.ShapeDtypeStruct((n_rows * chunks // 2, LANES), jnp.uint32),
        compiler_params=_params(dimension_semantics=("arbitrary",)),
        name="moe_grouped_matmul",
    )(pos, tile, ex, lo, hi, h_chunks, w_gu, w_down)


def _ple_kernel(pos_ref, h_ref, rw_ref, ys_ref, g_ref, b_ref, wgate_ref, p_ref, wp_ref, x_ref, xb_ref,
                ybuf, sems, wgate_b_ref, wp_b_ref, *, alpha):
    i = pl.program_id(0)
    n_tiles = pl.num_programs(0)
    tm = rw_ref.shape[0]
    tokens = pos_ref.shape[0] // 2
    yc = ybuf.shape[2] // tm
    hc = h_ref.shape[0] // tm

    def row_copies(tile, slot):
        return [pltpu.make_async_copy(ys_ref.at[pl.ds(pos_ref[s * tokens + tile * tm + r] * yc, yc), :],
                                      ybuf.at[slot, s, pl.ds(r * yc, yc), :], sems.at[slot])
                for r in range(tm) for s in range(2)]

    @pl.when(i == 0)
    def _():
        wgate_b_ref[...] = wgate_ref[...].astype(BF16)
        wp_b_ref[...] = wp_ref[...].astype(BF16)
        _start_alternating(row_copies(0, 0))

    slot = i % 2
    ahead = jnp.minimum(i + 1, n_tiles - 1)
    _start_alternating(row_copies(ahead, 1 - slot))
    for cp in row_copies(i, slot):
        cp.wait()
    rw = rw_ref[...]
    y0_hi, y0_lo = _unpack_bf16_pair(_load_row_chunks(ybuf.at[slot, 0], tm, yc))
    y1_hi, y1_lo = _unpack_bf16_pair(_load_row_chunks(ybuf.at[slot, 1], tm, yc))
    w0, w1 = rw[:, 0:1], rw[:, 1:2]
    f = jnp.concatenate([w0 * y0_hi + w1 * y1_hi, w0 * y0_lo + w1 * y1_lo], axis=1)
    h2 = _layer_norm(alpha * _load_row_chunks(h_ref, tm, hc) + f, g_ref[...], b_ref[...])
    gate = jax.nn.sigmoid(jnp.dot(h2.astype(BF16), wgate_b_ref[...], preferred_element_type=F32))
    pe = jnp.dot(p_ref[...].astype(BF16), wp_b_ref[...], preferred_element_type=F32)
    x_new = h2 + gate * pe
    x_ref[...] = x_new
    xb_ref[...] = x_new.astype(BF16)

    @pl.when(i == n_tiles - 1)
    def _():
        for cp in row_copies(ahead, 1 - slot):
            cp.wait()


def _ple(pos, h_chunks, rw, ys, ln_g, ln_b, w_gate, p, w_ple, layer, alpha, tm=256):
    d = w_gate.shape[-1]
    t = h_chunks.shape[0] * LANES // d
    pd = p.shape[-1]
    vec = pl.BlockSpec((None, 1, d), lambda i, pos: (layer, 0, 0))
    row = lambda w: pl.BlockSpec((tm, w), lambda i, pos: (i, 0))
    return pl.pallas_call(
        functools.partial(_ple_kernel, alpha=alpha),
        grid_spec=pltpu.PrefetchScalarGridSpec(
            num_scalar_prefetch=1,
            grid=(t // tm,),
            in_specs=[pl.BlockSpec((tm * d // LANES, LANES), lambda i, pos: (i, 0)),
                      row(ROUTER_LANES), pl.BlockSpec(memory_space=pl.ANY), vec, vec,
                      pl.BlockSpec((None, d, d), lambda i, pos: (layer, 0, 0), pipeline_mode=pl.Buffered(1)),
                      pl.BlockSpec((None, tm, pd), lambda i, pos: (layer, i, 0)),
                      pl.BlockSpec((None, pd, d), lambda i, pos: (layer, 0, 0), pipeline_mode=pl.Buffered(1))],
            out_specs=[row(d), row(d)],
            scratch_shapes=[pltpu.VMEM((2, 2, tm * d // (2 * LANES), LANES), jnp.uint32),
                            pltpu.SemaphoreType.DMA((2,)),
                            pltpu.VMEM((d, d), BF16), pltpu.VMEM((pd, d), BF16)]),
        out_shape=[jax.ShapeDtypeStruct((t, d), F32), jax.ShapeDtypeStruct((t, d), BF16)],
        compiler_params=_params(dimension_semantics=("arbitrary",)),
        name="gather_ln2_ple",
    )(pos, h_chunks, rw, ys, ln_g, ln_b, w_gate, p, w_ple)


def kernel(x, p, w_in, conv_w, conv_b, conv_ln_g, conv_ln_b, w_out, rel_bias, ln1_g, ln1_b, router_g_w, router_g_b, router_e_w, router_e_b, expert_w_gu, expert_w_down, ln2_g, ln2_b, ple_w, ple_gate_w):
    batch, seq, d = x.shape
    depth = w_in.shape[0]
    n_heads = rel_bias.shape[1]
    attn_w = n_heads * HEAD_DIM
    conv_width = conv_w.shape[2]
    t = batch * seq
    alpha = (2 * depth) ** 0.25

    pad = ROUTER_LANES - N_GROUPS - N_EXPERTS
    w_router = jnp.pad(jnp.concatenate([router_g_w, router_e_w], axis=-1), ((0, 0), (0, 0), (0, pad)))
    w_router_hi = w_router.astype(BF16)
    w_router_lo = (w_router - w_router_hi.astype(F32)).astype(BF16)
    w_router = jnp.concatenate([w_router_hi, w_router_lo], axis=-1)
    b_router = jnp.pad(jnp.concatenate([router_g_b, router_e_b], axis=-1), ((0, 0), (0, pad)))[:, None, :]
    vec3 = lambda v: v[:, None, :]
    p2 = p.reshape(depth, t, p.shape[-1])

    bias = _bias_tiles(rel_bias)
    xf = x.reshape(t, d)
    xb = xf.astype(BF16)
    for i in range(depth):
        qk = _proj_qk(xb, w_in, i, tn=attn_w)
        vt = _proj_vt(xb, w_in, i, batch, attn_w, col_block=2)
        u = _proj_glu(xb, w_in, i, a_col=3 * attn_w, width=conv_width)
        attn = _attention(qk, vt, bias, batch, n_heads)
        conv = _conv_module(u, conv_w, vec3(conv_b), vec3(conv_ln_g), vec3(conv_ln_b), i, batch)
        h, h_packed, route_i, route_w, cnt = _outproj(attn, conv, w_out, xf, vec3(ln1_g), vec3(ln1_b),
                                                      w_router, b_router, i, alpha)
        plan = _moe_plan(route_i, cnt, 2 * t)
        ys = _gmm(plan, h_packed, expert_w_gu, expert_w_down, i)
        xf, xb = _ple(plan[0], h, route_w, ys, vec3(ln2_g), vec3(ln2_b), ple_gate_w, p2, ple_w, i, alpha)
    return xf.reshape(batch, seq, d)
```
